```python
import math
import jax, jax.numpy as jnp
from jax import lax
import numpy as np

D_MODEL = 1024
BATCH = 8
SEQ = 8192
DEPTH = 4
DEC_BATCH = 2
DEC_SEQ = 16384
PAST_LEN = 128

HG_HEADS = 4
HG_DK = 128
HG_DV = 128
HG_WIDTH = HG_HEADS * HG_DV
HG_CHUNK = 64
AT_HEADS = 4
AT_HEAD_DIM = 64
AT_WIDTH = AT_HEADS * AT_HEAD_DIM
DIL_PATTERNS = ((128, 1), (512, 4), (2048, 16))
DIL_BLOCK = 64
ROPE_THETA = 500000.0
ROPE_DIM = AT_HEAD_DIM // 4
MEM_TOKENS = 256
MEM_HEADS = 4
MEM_HEAD_DIM = 64
MEM_WIDTH = MEM_HEADS * MEM_HEAD_DIM
MIX_WIDTH = HG_WIDTH + AT_WIDTH + MEM_WIDTH
NORM_EPS = 1e-6
MASK_VALUE = -1e30
SPLITS = (HG_WIDTH, HG_WIDTH, HG_WIDTH, HG_WIDTH, AT_WIDTH, AT_WIDTH, AT_WIDTH, MEM_WIDTH, HG_WIDTH, AT_WIDTH, MEM_WIDTH)
IN_WIDTH = 4 * HG_WIDTH + 3 * AT_WIDTH + MEM_WIDTH + MIX_WIDTH

kernel_name = 'hybrid_hgrn2_dilated_memory_encoder'


def _rms_norm(x, w):
    xf = x.astype(jnp.float32)
    y = xf * lax.rsqrt(jnp.mean(xf * xf, axis=-1, keepdims=True) + NORM_EPS)
    return (y * w.astype(jnp.float32)).astype(x.dtype)


def _head_norm(t, w):
    return t * lax.rsqrt(jnp.mean(t * t, axis=-1, keepdims=True) + NORM_EPS) * w.astype(jnp.float32)


def _partial_rope(t, pos):
    half = ROPE_DIM // 2
    inv_freq = ROPE_THETA ** (-jnp.arange(half, dtype=jnp.float32) * 2.0 / ROPE_DIM)
    ang = pos[:, None] * inv_freq[None, :]
    cos = jnp.cos(ang)[None, :, None, :]
    sin = jnp.sin(ang)[None, :, None, :]
    t1 = t[..., :half]
    t2 = t[..., half:ROPE_DIM]
    return jnp.concatenate([t1 * cos - t2 * sin, t2 * cos + t1 * sin, t[..., ROPE_DIM:]], axis=-1)


def _layer_lower_bounds(p):
    sm = jax.nn.softmax(p.astype(jnp.float32), axis=0)
    return jnp.cumsum(sm, axis=0) - sm[0:1]


def _hgrn2_chunk_scan(q, k, log_f, v):
    B, H, S, dk = q.shape
    dv = v.shape[-1]
    nc = S // HG_CHUNK

    def to_chunks(t):
        return t.reshape(B, H, nc, HG_CHUNK, t.shape[-1]).transpose(2, 0, 1, 3, 4)

    causal = jnp.tril(jnp.ones((HG_CHUNK, HG_CHUNK), dtype=bool))[None, None, :, :, None]

    def step(state, inp):
        qc, kc, lfc, vc = inp
        b = jnp.cumsum(lfc, axis=-2)
        b_last = b[:, :, -1:, :]
        o_inter = jnp.einsum('bhtk,bhkv->bhtv', qc * jnp.exp(b), state)
        diff = b[:, :, :, None, :] - b[:, :, None, :, :]
        decay = jnp.where(causal, jnp.exp(jnp.where(causal, diff, 0.0)), 0.0)
        scores = jnp.einsum('bhtk,bhsk,bhtsk->bhts', qc, kc, decay)
        o_intra = jnp.einsum('bhts,bhsv->bhtv', scores, vc)
        new_state = (jnp.exp(b_last[:, :, 0, :])[..., None] * state
                     + jnp.einsum('bhsk,bhsv->bhkv', kc * jnp.exp(b_last - b), vc))
        return new_state, o_inter + o_intra

    state0 = jnp.zeros((B, H, dk, dv), jnp.float32)
    _, o = lax.scan(step, state0, (to_chunks(q), to_chunks(k), to_chunks(log_f), to_chunks(v)))
    return o.transpose(1, 2, 0, 3, 4).reshape(B, H, S, dv)


def _dilated_branch(q, k, v, window, dilation):
    B, S, H, Dh = q.shape
    half = window // (2 * dilation)
    L = S // dilation
    nb = -(-L // DIL_BLOCK)
    Lp = nb * DIL_BLOCK

    def to_sub(t):
        return t.reshape(B, L, dilation, H, Dh).transpose(0, 2, 3, 1, 4)

    qs = jnp.pad(to_sub(q), ((0, 0), (0, 0), (0, 0), (0, Lp - L), (0, 0))).reshape(B, dilation, H, nb, DIL_BLOCK, Dh)

    def neighbour_blocks(t):
        tp = jnp.pad(to_sub(t), ((0, 0), (0, 0), (0, 0), (DIL_BLOCK, DIL_BLOCK + Lp - L), (0, 0)))
        tp = tp.reshape(B, dilation, H, nb + 2, DIL_BLOCK, Dh)
        return jnp.concatenate([tp[:, :, :, :-2], tp[:, :, :, 1:-1], tp[:, :, :, 2:]], axis=-2)

    kb = neighbour_blocks(k)
    vb = neighbour_blocks(v)
    qi = (jnp.arange(nb)[:, None] * DIL_BLOCK + jnp.arange(DIL_BLOCK)[None, :])[:, :, None]
    kj = (jnp.arange(nb)[:, None] * DIL_BLOCK - DIL_BLOCK + jnp.arange(3 * DIL_BLOCK)[None, :])[:, None, :]
    valid = (jnp.abs(kj - qi) <= half) & (kj >= 0) & (kj < L)
    s = jnp.einsum('brhnqd,brhnkd->brhnqk', qs, kb) * (Dh ** -0.5)
    s = jnp.where(valid, s, MASK_VALUE)
    lse = jax.nn.logsumexp(s, axis=-1)
    p = jnp.exp(s - lse[..., None])
    o = jnp.einsum('brhnqk,brhnkd->brhnqd', p, vb)

    def from_sub(t):
        t = t.reshape((B, dilation, H, Lp) + t.shape[5:])[:, :, :, :L]
        t = jnp.moveaxis(t, 3, 1)
        return t.reshape((B, S, H) + t.shape[4:])

    return from_sub(o), from_sub(lse)


def _layer(x, mem, pos, lb_f, lb_b, norm_w, w_in, hg_onorm_w, aq_w, ak_w, mem_norm_w, mem_wkv, mq_w, mk_w, w_out):
    B, S, _ = x.shape
    f32 = jnp.float32
    h = _rms_norm(x, norm_w)
    z = jnp.matmul(h, w_in).astype(f32)
    points = [int(p) for p in np.cumsum(SPLITS)[:-1]]
    hq, hf_fwd, hf_bwd, hi, aq, ak, av, mq, g_hg, g_at, g_mem = jnp.split(z, points, axis=-1)

    def hg_heads(t, d):
        return t.reshape(B, S, HG_HEADS, d).transpose(0, 2, 1, 3)

    q = jax.nn.silu(hg_heads(hq, HG_DK))
    v = hg_heads(hi, HG_DV)

    def gate_terms(zf, lb):
        lb = lb.astype(f32).reshape(1, HG_HEADS, 1, HG_DK)
        f = lb + (1.0 - lb) * jax.nn.sigmoid(zf)
        log_f = jnp.log(f)
        kk = (1.0 - lb) * jax.nn.sigmoid(-zf)
        return log_f, kk

    logf_f, k_f = gate_terms(hg_heads(hf_fwd, HG_DK), lb_f)
    logf_b, k_b = gate_terms(hg_heads(hf_bwd, HG_DK), lb_b)
    o_fwd = _hgrn2_chunk_scan(q, k_f, logf_f, v)
    flip = lambda t: jnp.flip(t, axis=2)
    o_bwd = flip(_hgrn2_chunk_scan(flip(q), flip(k_b), flip(logf_b), flip(v)))
    o_hg = _head_norm((o_fwd + o_bwd).transpose(0, 2, 1, 3), hg_onorm_w).reshape(B, S, HG_WIDTH)
    o_hg = o_hg * jax.nn.silu(g_hg)

    qa = _partial_rope(_head_norm(aq.reshape(B, S, AT_HEADS, AT_HEAD_DIM), aq_w), pos)
    ka = _partial_rope(_head_norm(ak.reshape(B, S, AT_HEADS, AT_HEAD_DIM), ak_w), pos)
    va = av.reshape(B, S, AT_HEADS, AT_HEAD_DIM)
    outs = []
    lses = []
    for window, dilation in DIL_PATTERNS:
        o_i, lse_i = _dilated_branch(qa, ka, va, window, dilation)
        outs.append(o_i)
        lses.append(lse_i)
    wts = jax.nn.softmax(jnp.stack(lses, axis=0), axis=0)
    o_at = jnp.sum(wts[..., None] * jnp.stack(outs, axis=0), axis=0).reshape(B, S, AT_WIDTH)
    o_at = o_at * jax.nn.silu(g_at)

    M = mem.shape[1]
    mh = _rms_norm(mem, mem_norm_w)
    mkv = jnp.matmul(mh, mem_wkv).astype(f32)
    mk = _head_norm(mkv[..., :MEM_WIDTH].reshape(B, M, MEM_HEADS, MEM_HEAD_DIM), mk_w)
    mv = mkv[..., MEM_WIDTH:].reshape(B, M, MEM_HEADS, MEM_HEAD_DIM)
    mqh = _head_norm(mq.reshape(B, S, MEM_HEADS, MEM_HEAD_DIM), mq_w)
    sm = jnp.einsum('bshd,bmhd->bhsm', mqh, mk) * (MEM_HEAD_DIM ** -0.5)
    pm = jax.nn.softmax(sm, axis=-1)
    o_mem = jnp.einsum('bhsm,bmhd->bshd', pm, mv).reshape(B, S, MEM_WIDTH)
    o_mem = o_mem * jax.nn.silu(g_mem)

    mixed = jnp.concatenate([o_hg, o_at, o_mem], axis=-1).astype(x.dtype)
    return x + jnp.matmul(mixed, w_out)


def setup_inputs(seed: int = 0) -> dict:
    key = jax.random.key(seed)
    ks = jax.random.split(key, 16)
    f32 = jnp.float32
    nrm = lambda k, shape, scale: scale * jax.random.normal(k, shape, f32)
    return {
        'x_prompt': nrm(ks[0], (BATCH, SEQ, D_MODEL), 1.0),
        'x_sample': nrm(ks[1], (DEC_BATCH, DEC_SEQ, D_MODEL), 1.0),
        'mem_prompt': nrm(ks[2], (BATCH, MEM_TOKENS, D_MODEL), 1.0),
        'mem_sample': nrm(ks[3], (DEC_BATCH, MEM_TOKENS, D_MODEL), 1.0),
        'norm_w': 1.0 + nrm(ks[4], (DEPTH, D_MODEL), 0.02),
        'w_in': nrm(ks[5], (DEPTH, D_MODEL, IN_WIDTH), D_MODEL ** -0.5),
        'hgrn_lb_fwd': nrm(ks[6], (DEPTH, HG_WIDTH), 0.1),
        'hgrn_lb_bwd': nrm(ks[7], (DEPTH, HG_WIDTH), 0.1),
        'hgrn_onorm_w': 1.0 + nrm(ks[8], (DEPTH, HG_DV), 0.02),
        'attn_qnorm_w': 1.0 + nrm(ks[9], (DEPTH, AT_HEAD_DIM), 0.02),
        'attn_knorm_w': 1.0 + nrm(ks[10], (DEPTH, AT_HEAD_DIM), 0.02),
        'mem_norm_w': 1.0 + nrm(ks[11], (DEPTH, D_MODEL), 0.02),
        'mem_wkv': nrm(ks[12], (DEPTH, D_MODEL, 2 * MEM_WIDTH), D_MODEL ** -0.5),
        'mem_qnorm_w': 1.0 + nrm(ks[13], (DEPTH, MEM_HEAD_DIM), 0.02),
        'mem_knorm_w': 1.0 + nrm(ks[14], (DEPTH, MEM_HEAD_DIM), 0.02),
        'w_out': nrm(ks[15], (DEPTH, MIX_WIDTH, D_MODEL), MIX_WIDTH ** -0.5),
    }


def reference(x_prompt, x_sample, mem_prompt, mem_sample, norm_w, w_in, hgrn_lb_fwd, hgrn_lb_bwd, hgrn_onorm_w, attn_qnorm_w, attn_knorm_w, mem_norm_w, mem_wkv, mem_qnorm_w, mem_knorm_w, w_out):
    lb_fwd = _layer_lower_bounds(hgrn_lb_fwd)
    lb_bwd = _layer_lower_bounds(hgrn_lb_bwd)

    def trunk(x, mem):
        pos = jnp.arange(x.shape[1], dtype=jnp.float32)
        for l in range(DEPTH):
            x = _layer(x, mem, pos, lb_fwd[l], lb_bwd[l], norm_w[l], w_in[l], hgrn_onorm_w[l],
                       attn_qnorm_w[l], attn_knorm_w[l], mem_norm_w[l], mem_wkv[l],
                       mem_qnorm_w[l], mem_knorm_w[l], w_out[l])
        return x

    y_prompt = trunk(x_prompt, mem_prompt)
    y_sample = trunk(x_sample, mem_sample)
    return (y_prompt, y_sample)
```

```python
import functools
import math

import jax
import jax.numpy as jnp
from jax import lax
from jax.experimental import pallas as pl
from jax.experimental.pallas import tpu as pltpu

F32 = jnp.float32
BF16 = jnp.bfloat16

D_MODEL = 1024
DEPTH = 4
HG_HEADS = 4
HG_D = 128
HG_WIDTH = HG_HEADS * HG_D
AT_WIDTH = 256
HEAD64 = 64
MEM_WIDTH = 256
MEM_TOKENS = 256
IN_WIDTH = 4096
DILATIONS = (1, 4, 16)
BAND = 64
ROPE_THETA = 500000.0
ROPE_DIM = 16
NORM_EPS = 1e-6
MASK_VALUE = -1e30
LANES = 128

C_HQ, C_FF, C_FB, C_HI = 0, 512, 1024, 1536
C_AQ, C_AK, C_AV, C_MQ = 2048, 2304, 2560, 2816
C_GATE = 3072

SCAN_CHUNK = 64
EXP_CLAMP = 80.0
VMEM_LIMIT = 56 * 1024 * 1024


def _cparams(*sem):
    return pltpu.CompilerParams(dimension_semantics=sem, vmem_limit_bytes=VMEM_LIMIT)


def _sigmoid(z):
    return 1.0 / (1.0 + jnp.exp(-z))


def _lane_iota(n=LANES):
    return lax.broadcasted_iota(jnp.int32, (1, n), 1)


def _headnorm64(t, w):
    lane = _lane_iota()
    lo_mask = lane < HEAD64
    parts = []
    for hp in range(t.shape[1] // LANES):
        th = t[:, hp * LANES:(hp + 1) * LANES]
        t2 = th * th
        lo = jnp.sum(jnp.where(lo_mask, t2, 0.0), axis=-1, keepdims=True)
        hi = jnp.sum(jnp.where(lo_mask, 0.0, t2), axis=-1, keepdims=True)
        ms = jnp.where(lo_mask, lo, hi) * (1.0 / HEAD64)
        parts.append(th * lax.rsqrt(ms + NORM_EPS))
    return jnp.concatenate(parts, axis=-1) * w


def _rope(t, cos_t, sin_t):
    lane = _lane_iota() % HEAD64
    parts = []
    for hp in range(t.shape[1] // LANES):
        th = t[:, hp * LANES:(hp + 1) * LANES]
        partner = jnp.where(lane < ROPE_DIM // 2,
                            pltpu.roll(th, LANES - ROPE_DIM // 2, 1),
                            pltpu.roll(th, ROPE_DIM // 2, 1))
        parts.append(th * cos_t + partner * sin_t)
    return jnp.concatenate(parts, axis=-1)


def _in_proj_kernel(x_ref, nw_ref, w_ref, lbf_ref, lbb_ref, aqw_ref, akw_ref, mqw_ref,
                    cos_ref, sin_ref,
                    hq_o, hv_o, lff_o, kkf_o, lfb_o, kkb_o, qa_o, ka_o, va_o, mq_o, g_o):
    x = x_ref[...]
    ms = jnp.mean(x * x, axis=-1, keepdims=True)
    h = (x * lax.rsqrt(ms + NORM_EPS) * nw_ref[...]).astype(BF16)

    def proj(c0, n):
        return jnp.dot(h, w_ref[:, c0:c0 + n], preferred_element_type=F32)

    z = proj(C_HQ, HG_WIDTH)
    hq_o[...] = (z * _sigmoid(z)).astype(BF16)
    hv_o[...] = proj(C_HI, HG_WIDTH).astype(BF16)

    def gate_terms(c0, lb_ref, lf_o, kk_o):
        lb = lb_ref[...]
        sig = _sigmoid(proj(c0, HG_WIDTH))
        lf_o[...] = jnp.log(lb + (1.0 - lb) * sig)
        kk_o[...] = ((1.0 - lb) * (1.0 - sig)).astype(BF16)

    gate_terms(C_FF, lbf_ref, lff_o, kkf_o)
    gate_terms(C_FB, lbb_ref, lfb_o, kkb_o)

    cos_t = cos_ref[...]
    sin_t = sin_ref[...]
    scale = HEAD64 ** -0.5
    qa = _rope(_headnorm64(proj(C_AQ, AT_WIDTH), aqw_ref[...]), cos_t, sin_t)
    qa_o[...] = (qa * scale).astype(BF16)
    ka_o[...] = _rope(_headnorm64(proj(C_AK, AT_WIDTH), akw_ref[...]), cos_t, sin_t).astype(BF16)
    va_o[...] = proj(C_AV, AT_WIDTH).astype(BF16)
    mq_o[...] = (_headnorm64(proj(C_MQ, MEM_WIDTH), mqw_ref[...]) * scale).astype(BF16)
    for j in range(2):
        g = proj(C_GATE + j * 512, 512)
        g_o[:, j * 512:(j + 1) * 512] = (g * _sigmoid(g)).astype(BF16)


def _in_proj(x, nw, w_in, lbf, lbb, aqw, akw, mqw, cos_t, sin_t, ts):
    B, S, _ = x.shape
    row = lambda n: pl.BlockSpec((None, ts, n), lambda b, i: (b, i, 0))
    vec = lambda n: pl.BlockSpec((1, n), lambda b, i: (0, 0))
    tab = pl.BlockSpec((ts, LANES), lambda b, i: (i, 0))
    out_widths = [(HG_WIDTH, BF16), (HG_WIDTH, BF16), (HG_WIDTH, F32), (HG_WIDTH, BF16),
                  (HG_WIDTH, F32), (HG_WIDTH, BF16), (AT_WIDTH, BF16), (AT_WIDTH, BF16),
                  (AT_WIDTH, BF16), (MEM_WIDTH, BF16), (1024, BF16)]
    return pl.pallas_call(
        _in_proj_kernel,
        grid=(B, S // ts),
        in_specs=[row(D_MODEL), vec(D_MODEL),
                  pl.BlockSpec((D_MODEL, IN_WIDTH), lambda b, i: (0, 0)),
                  vec(HG_WIDTH), vec(HG_WIDTH), vec(AT_WIDTH), vec(AT_WIDTH), vec(MEM_WIDTH),
                  tab, tab],
        out_specs=[row(n) for n, _ in out_widths],
        out_shape=[jax.ShapeDtypeStruct((B, S, n), dt) for n, dt in out_widths],
        compiler_params=_cparams("parallel", "parallel"),
        name="in_proj",
    )(x, nw, w_in, lbf, lbb, aqw, akw, mqw, cos_t, sin_t)


def _scan_kernel(q_ref, v_ref, kk_ref, lf_ref, o_ref, st_ref, b_ref, *, reverse, ts):
    C = SCAN_CHUNK
    n_chunks = ts // C

    @pl.when(pl.program_id(1) == 0)
    def _():
        st_ref[...] = jnp.zeros_like(st_ref)

    lf = lf_ref[...]
    pos = lax.broadcasted_iota(jnp.int32, (ts, 1), 0) % C
    s = 1
    while s < C:
        if reverse:
            lf = lf + jnp.where(pos < C - s, pltpu.roll(lf, ts - s, 0), 0.0)
        else:
            lf = lf + jnp.where(pos >= s, pltpu.roll(lf, s, 0), 0.0)
        s *= 2
    b_ref[...] = lf

    ri = lax.broadcasted_iota(jnp.int32, (C, C), 0)
    ci = lax.broadcasted_iota(jnp.int32, (C, C), 1)
    tri = (ci >= ri) if reverse else (ci <= ri)
    end_row = 0 if reverse else C - 1
    mid_row = C // 2 if reverse else C // 2 - 1

    def chunk(ci_, carry):
        c = (n_chunks - 1 - ci_) if reverse else ci_
        r0 = pl.multiple_of(c * C, C)
        rows = pl.ds(r0, C)
        for hd in range(HG_HEADS):
            cols = slice(hd * HG_D, (hd + 1) * HG_D)
            b = b_ref[rows, cols]
            q = q_ref[rows, cols].astype(F32)
            k = kk_ref[rows, cols].astype(F32)
            v = v_ref[rows, cols]
            b_end = b[end_row:end_row + 1, :]
            b_mid = b[mid_row:mid_row + 1, :]
            q_inter = (q * jnp.exp(b)).astype(BF16)
            k_state = (k * jnp.exp(b_end - b)).astype(BF16)
            q_intra = (q * jnp.exp(jnp.minimum(b - b_mid, EXP_CLAMP))).astype(BF16)
            k_intra = (k * jnp.exp(jnp.minimum(b_mid - b, EXP_CLAMP))).astype(BF16)
            sc = lax.dot_general(q_intra, k_intra, (((1,), (1,)), ((), ())),
                                 preferred_element_type=F32)
            sc = jnp.where(tri, sc, 0.0).astype(BF16)
            st = st_ref[hd]
            o = lax.dot_general(q_inter, st.astype(BF16), (((1,), (1,)), ((), ())),
                                preferred_element_type=F32)
            o = o + jnp.dot(sc, v, preferred_element_type=F32)
            o_ref[rows, cols] = o
            upd = lax.dot_general(v, k_state, (((0,), (0,)), ((), ())),
                                  preferred_element_type=F32)
            st_ref[hd] = st * jnp.exp(b_end) + upd
        return carry

    lax.fori_loop(0, n_chunks, chunk, 0)


def _scan(q, v, kk, lf, reverse, ts):
    B, S, _ = q.shape
    nt = S // ts
    if reverse:
        imap = lambda b, i: (b, nt - 1 - i, 0)
    else:
        imap = lambda b, i: (b, i, 0)
    spec = pl.BlockSpec((None, ts, HG_WIDTH), imap)
    return pl.pallas_call(
        functools.partial(_scan_kernel, reverse=reverse, ts=ts),
        grid=(B, nt),
        in_specs=[spec, spec, spec, spec],
        out_specs=spec,
        out_shape=jax.ShapeDtypeStruct((B, S, HG_WIDTH), F32),
        scratch_shapes=[pltpu.VMEM((HG_HEADS, HG_D, HG_D), F32),
                        pltpu.VMEM((ts, HG_WIDTH), F32)],
        compiler_params=_cparams("parallel", "arbitrary"),
        name="scan_bwd" if reverse else "scan_fwd",
    )(q, v, kk, lf)


def _mem_kv_kernel(m_ref, nw_ref, w_ref, kw_ref, mk_o, mv_o):
    x = m_ref[...]
    ms = jnp.mean(x * x, axis=-1, keepdims=True)
    h = (x * lax.rsqrt(ms + NORM_EPS) * nw_ref[...]).astype(BF16)
    kv = jnp.dot(h, w_ref[...], preferred_element_type=F32)
    mk_o[...] = _headnorm64(kv[:, :MEM_WIDTH], kw_ref[...]).astype(BF16)
    mv_o[...] = kv[:, MEM_WIDTH:].astype(BF16)


def _mem_kv(mem, nw, wkv, kw):
    B, M, _ = mem.shape
    out = pl.BlockSpec((None, M, MEM_WIDTH), lambda b: (b, 0, 0))
    return pl.pallas_call(
        _mem_kv_kernel,
        grid=(B,),
        in_specs=[pl.BlockSpec((None, M, D_MODEL), lambda b: (b, 0, 0)),
                  pl.BlockSpec((1, D_MODEL), lambda b: (0, 0)),
                  pl.BlockSpec((D_MODEL, 2 * MEM_WIDTH), lambda b: (0, 0)),
                  pl.BlockSpec((1, MEM_WIDTH), lambda b: (0, 0))],
        out_specs=[out, out],
        out_shape=[jax.ShapeDtypeStruct((B, M, MEM_WIDTH), BF16)] * 2,
        compiler_params=_cparams("parallel"),
        name="mem_kv",
    )(mem, nw, wkv, kw)


QBLK = 128


def _softmax_pv(qm, k2, v2, mask):
    s = lax.dot_general(qm, k2, (((1,), (1,)), ((), ())), preferred_element_type=F32)
    s = jnp.where(mask, s, MASK_VALUE)
    m = jnp.max(s, axis=-1, keepdims=True)
    p = jnp.exp(s - m)
    l = jnp.sum(p, axis=-1, keepdims=True)
    o = jnp.dot(p.astype(BF16), v2, preferred_element_type=F32)
    return o / l, m + jnp.log(l)


def _band_attn_kernel(q_ref, kp_ref, km_ref, kn_ref, vp_ref, vm_ref, vn_ref, o_ref, lse_ref,
                      *, rows, n_rows_total, dil):
    i = pl.program_id(1)
    lane = _lane_iota()
    lo_mask = lane < HEAD64
    qi = lax.broadcasted_iota(jnp.int32, (QBLK, QBLK + 2 * BAND), 0)
    ke = lax.broadcasted_iota(jnp.int32, (QBLK, QBLK + 2 * BAND), 1)
    delta = ke - BAND - qi
    band = (delta <= BAND) & (delta >= -BAND)
    for jb in range(rows // QBLK):
        a = jb * QBLK
        grow = i * rows + (a - BAND) + ke
        mask = band & (grow >= 0) & (grow < n_rows_total)
        for cb in range(dil * AT_WIDTH // LANES):
            cols = slice(cb * LANES, (cb + 1) * LANES)

            def window(p_ref, m_ref, n_ref):
                head = p_ref[:, cols] if a == 0 else m_ref[a - BAND:a, cols]
                tail = n_ref[:, cols] if a + QBLK == rows else m_ref[a + QBLK:a + QBLK + BAND, cols]
                return jnp.concatenate([head, m_ref[a:a + QBLK, cols], tail], axis=0)

            k2 = window(kp_ref, km_ref, kn_ref)
            v2 = window(vp_ref, vm_ref, vn_ref)
            q2 = q_ref[a:a + QBLK, cols]
            zero = jnp.zeros_like(q2)
            o_lo, lse_lo = _softmax_pv(jnp.where(lo_mask, q2, zero), k2, v2, mask)
            o_hi, lse_hi = _softmax_pv(jnp.where(lo_mask, zero, q2), k2, v2, mask)
            o_ref[a:a + QBLK, cols] = jnp.where(lo_mask, o_lo, o_hi).astype(BF16)
            lse_ref[a:a + QBLK, cols] = jnp.where(lo_mask, lse_lo, lse_hi)


def _band_attn(q, k, v, dil, rows):
    B, S, _ = q.shape
    L = S // dil
    W = dil * AT_WIDTH
    qv, kv, vv = (t.reshape(B, L, W) for t in (q, k, v))
    nblk = rows // BAND
    last = L // BAND - 1
    main = pl.BlockSpec((None, rows, W), lambda b, i: (b, i, 0))
    prev = pl.BlockSpec((None, BAND, W), lambda b, i: (b, jnp.maximum(i * nblk - 1, 0), 0))
    nxt = pl.BlockSpec((None, BAND, W), lambda b, i: (b, jnp.minimum((i + 1) * nblk, last), 0))
    o, lse = pl.pallas_call(
        functools.partial(_band_attn_kernel, rows=rows, n_rows_total=L, dil=dil),
        grid=(B, L // rows),
        in_specs=[main, prev, main, nxt, prev, main, nxt],
        out_specs=[main, main],
        out_shape=[jax.ShapeDtypeStruct((B, L, W), BF16), jax.ShapeDtypeStruct((B, L, W), F32)],
        compiler_params=_cparams("parallel", "parallel"),
        name=f"band_attn_d{dil}",
    )(qv, kv, kv, kv, vv, vv, vv)
    return o.reshape(B, S, AT_WIDTH), lse.reshape(B, S, AT_WIDTH)


def _out_kernel(x_ref, of_ref, ob_ref, g_ref, o1_ref, o4_ref, o16_ref, l1_ref, l4_ref, l16_ref,
                mq_ref, mk_ref, mv_ref, onw_ref, w_ref, y_ref):
    g = g_ref[...].astype(F32)
    parts = []
    for hd in range(HG_HEADS):
        cols = slice(hd * HG_D, (hd + 1) * HG_D)
        t = of_ref[:, cols] + ob_ref[:, cols]
        ms = jnp.mean(t * t, axis=-1, keepdims=True)
        parts.append(t * lax.rsqrt(ms + NORM_EPS) * onw_ref[...] * g[:, cols])
    l1, l4, l16 = l1_ref[...], l4_ref[...], l16_ref[...]
    m = jnp.maximum(jnp.maximum(l1, l4), l16)
    e1, e4, e16 = jnp.exp(l1 - m), jnp.exp(l4 - m), jnp.exp(l16 - m)
    o_at = (e1 * o1_ref[...].astype(F32) + e4 * o4_ref[...].astype(F32)
            + e16 * o16_ref[...].astype(F32)) / (e1 + e4 + e16)
    parts.append(o_at * g[:, HG_WIDTH:HG_WIDTH + AT_WIDTH])
    lane = _lane_iota()
    lo_mask = lane < HEAD64
    for hp in range(MEM_WIDTH // LANES):
        cols = slice(hp * LANES, (hp + 1) * LANES)
        q2 = mq_ref[:, cols]
        k2 = mk_ref[:, cols]
        v2 = mv_ref[:, cols]
        zero = jnp.zeros_like(q2)
        o_lo, _ = _softmax_pv(jnp.where(lo_mask, q2, zero), k2, v2, True)
        o_hi, _ = _softmax_pv(jnp.where(lo_mask, zero, q2), k2, v2, True)
        gc = slice(HG_WIDTH + AT_WIDTH + hp * LANES, HG_WIDTH + AT_WIDTH + (hp + 1) * LANES)
        parts.append(jnp.where(lo_mask, o_lo, o_hi) * g[:, gc])
    mixed = jnp.concatenate(parts, axis=-1).astype(BF16)
    y_ref[...] = x_ref[...] + jnp.dot(mixed, w_ref[...], preferred_element_type=F32)


def _out_stage(x, o_f, o_b, g, o_pat, lse_pat, mq, mk, mv, onw, w_out, ts):
    B, S, _ = x.shape
    row = lambda n: pl.BlockSpec((None, ts, n), lambda b, i: (b, i, 0))
    memspec = pl.BlockSpec((None, MEM_TOKENS, MEM_WIDTH), lambda b, i: (b, 0, 0))
    return pl.pallas_call(
        _out_kernel,
        grid=(B, S // ts),
        in_specs=[row(D_MODEL), row(HG_WIDTH), row(HG_WIDTH), row(1024),
                  row(AT_WIDTH), row(AT_WIDTH), row(AT_WIDTH),
                  row(AT_WIDTH), row(AT_WIDTH), row(AT_WIDTH),
                  row(MEM_WIDTH), memspec, memspec,
                  pl.BlockSpec((1, HG_D), lambda b, i: (0, 0)),
                  pl.BlockSpec((D_MODEL, D_MODEL), lambda b, i: (0, 0))],
        out_specs=row(D_MODEL),
        out_shape=jax.ShapeDtypeStruct((B, S, D_MODEL), F32),
        compiler_params=_cparams("parallel", "parallel"),
        name="out_stage",
    )(x, o_f, o_b, g, *o_pat, *lse_pat, mq, mk, mv, onw, w_out)


def _rope_tables(S):
    half = ROPE_DIM // 2
    pos = jnp.arange(S, dtype=F32)
    inv_freq = ROPE_THETA ** (-jnp.arange(half, dtype=F32) * 2.0 / ROPE_DIM)
    ang = pos[:, None] * inv_freq[None, :]
    cos, sin = jnp.cos(ang), jnp.sin(ang)
    pad = HEAD64 - ROPE_DIM
    cos64 = jnp.concatenate([cos, cos, jnp.ones((S, pad), F32)], axis=-1)
    sin64 = jnp.concatenate([-sin, sin, jnp.zeros((S, pad), F32)], axis=-1)
    return jnp.tile(cos64, (1, 2)), jnp.tile(sin64, (1, 2))


def _lower_bounds(p):
    sm = jax.nn.softmax(p.astype(F32), axis=0)
    return jnp.cumsum(sm, axis=0) - sm[0:1]


def _tile4(w):
    return jnp.tile(w.astype(F32), (1, 4))


def _trunk(x, mem, prm, ts_proj, ts_scan, attn_rows):
    B, S, _ = x.shape
    cos_t, sin_t = _rope_tables(S)
    for l in range(DEPTH):
        p = {k: v[l] for k, v in prm.items()}
        (hq, hv, lff, kkf, lfb, kkb, qa, ka, va, mq, g) = _in_proj(
            x, p["norm_w"], p["w_in"], p["lbf"], p["lbb"], p["aqw"], p["akw"], p["mqw"],
            cos_t, sin_t, ts_proj)
        o_f = _scan(hq, hv, kkf, lff, False, ts_scan)
        o_b = _scan(hq, hv, kkb, lfb, True, ts_scan)
        mk, mv = _mem_kv(mem, p["mem_norm_w"], p["mem_wkv"], p["mkw"])
        o_pat, lse_pat = [], []
        for dil in DILATIONS:
            o_i, lse_i = _band_attn(qa, ka, va, dil, attn_rows[dil])
            o_pat.append(o_i)
            lse_pat.append(lse_i)
        x = _out_stage(x, o_f, o_b, g, o_pat, lse_pat, mq, mk, mv, p["onw"], p["w_out"], ts_proj)
    return x


def kernel(x_prompt, x_sample, mem_prompt, mem_sample, norm_w, w_in, hgrn_lb_fwd, hgrn_lb_bwd,
           hgrn_onorm_w, attn_qnorm_w, attn_knorm_w, mem_norm_w, mem_wkv, mem_qnorm_w,
           mem_knorm_w, w_out):
    prm = {
        "norm_w": norm_w.astype(F32)[:, None, :],
        "w_in": w_in.astype(BF16),
        "lbf": _lower_bounds(hgrn_lb_fwd)[:, None, :],
        "lbb": _lower_bounds(hgrn_lb_bwd)[:, None, :],
        "onw": hgrn_onorm_w.astype(F32)[:, None, :],
        "aqw": _tile4(attn_qnorm_w)[:, None, :],
        "akw": _tile4(attn_knorm_w)[:, None, :],
        "mem_norm_w": mem_norm_w.astype(F32)[:, None, :],
        "mem_wkv": mem_wkv.astype(BF16),
        "mqw": _tile4(mem_qnorm_w)[:, None, :],
        "mkw": _tile4(mem_knorm_w)[:, None, :],
        "w_out": w_out.astype(BF16),
    }
    attn_rows = {1: 512, 4: 256, 16: 128}
    y_prompt = _trunk(x_prompt, mem_prompt, prm, 512, 512, attn_rows)
    y_sample = _trunk(x_sample, mem_sample, prm, 512, 512, attn_rows)
    return (y_prompt, y_sample)
```

```python
import functools

import jax
import jax.numpy as jnp
from jax import lax
from jax.experimental import pallas as pl
from jax.experimental.pallas import tpu as pltpu

F32 = jnp.float32
BF16 = jnp.bfloat16

D_MODEL = 1024
DEPTH = 4
HG_HEADS = 4
HG_D = 128
HG_WIDTH = HG_HEADS * HG_D
AT_WIDTH = 256
HEAD64 = 64
MEM_WIDTH = 256
MEM_TOKENS = 256
IN_WIDTH = 4096
DILATIONS = (1, 4, 16)
BAND = 64
ROPE_THETA = 500000.0
ROPE_DIM = 16
NORM_EPS = 1e-6
MASK_VALUE = -1e30
LANES = 128

C_HQ, C_FF, C_FB, C_HI = 0, 512, 1024, 1536
C_AQ, C_AK, C_AV, C_MQ = 2048, 2304, 2560, 2816
C_GATE = 3072

LOG2E = 1.4426950408889634
EXP2_RANGE = 63.0
PROJ_TILE = 512
ATT_TILE = 2048
QBLK = 128
ATT_GROUP = 4
SCAN_TILE = 512
SCAN_CHUNK = 64
EXP_CLAMP = 80.0
VMEM_LIMIT = 56 * 1024 * 1024


def _cparams(*sem):
    return pltpu.CompilerParams(dimension_semantics=sem, vmem_limit_bytes=VMEM_LIMIT)


def _sigmoid(z):
    return 1.0 / (1.0 + jnp.exp(-z))


def _lane_iota(n=LANES):
    return lax.broadcasted_iota(jnp.int32, (1, n), 1)


def _headnorm64(t, w):
    lo_mask = _lane_iota() < HEAD64
    parts = []
    for hp in range(t.shape[1] // LANES):
        th = t[:, hp * LANES:(hp + 1) * LANES]
        t2 = th * th
        lo = jnp.sum(jnp.where(lo_mask, t2, 0.0), axis=-1, keepdims=True)
        hi = jnp.sum(jnp.where(lo_mask, 0.0, t2), axis=-1, keepdims=True)
        ms = jnp.where(lo_mask, lo, hi) * (1.0 / HEAD64)
        parts.append(th * lax.rsqrt(ms + NORM_EPS))
    return jnp.concatenate(parts, axis=-1) * w


def _rope(t, cos_t, sin_t):
    lane = _lane_iota() % HEAD64
    parts = []
    for hp in range(t.shape[1] // LANES):
        th = t[:, hp * LANES:(hp + 1) * LANES]
        partner = jnp.where(lane < ROPE_DIM // 2,
                            pltpu.roll(th, LANES - ROPE_DIM // 2, 1),
                            pltpu.roll(th, ROPE_DIM // 2, 1))
        parts.append(th * cos_t + partner * sin_t)
    return jnp.concatenate(parts, axis=-1)


def _pair_attend(q2, k2, va, vb, bias):
    return _pairs_attend([(q2, k2, va, vb)], bias)[0]


def _pairs_attend(pairs, bias):
    lo_mask = _lane_iota() < HEAD64
    nt = (((1,), (1,)), ((), ()))
    s = []
    for q2, k2, _, _ in pairs:
        zero = jnp.zeros_like(q2)
        s.append((lax.dot_general(jnp.where(lo_mask, q2, zero), k2, nt, preferred_element_type=F32),
                  lax.dot_general(jnp.where(lo_mask, zero, q2), k2, nt, preferred_element_type=F32)))
    p = [(jnp.exp2(s_lo + bias).astype(BF16), jnp.exp2(s_hi + bias).astype(BF16)) for s_lo, s_hi in s]
    o = [(jnp.dot(p_lo, va, preferred_element_type=F32), jnp.dot(p_hi, vb, preferred_element_type=F32))
         for (p_lo, p_hi), (_, _, va, vb) in zip(p, pairs)]
    return [(jnp.where(lo_mask, o_lo, o_hi), pltpu.roll(jnp.where(lo_mask, o_hi, o_lo), HEAD64, 1))
            for o_lo, o_hi in o]


def _in_proj_kernel(x_ref, nw_ref, w_ref, lbf_ref, lbb_ref, aqw_ref, akw_ref, mqw_ref,
                    cos_ref, sin_ref,
                    hq_o, hv_o, lff_o, kkf_o, lfb_o, kkb_o,
                    q1_o, q4_o, q16_o, k1_o, k4_o, k16_o, v1_o, v4_o, v16_o, mq_o, g_o,
                    tmp_lo, tmp_hi):
    ts = x_ref.shape[0]
    x = x_ref[...]
    ms = jnp.mean(x * x, axis=-1, keepdims=True)
    h = (x * lax.rsqrt(ms + NORM_EPS) * nw_ref[...]).astype(BF16)

    def proj(c0, n):
        return jnp.dot(h, w_ref[:, c0:c0 + n], preferred_element_type=F32)

    z = proj(C_HQ, HG_WIDTH)
    hq_o[...] = (z * _sigmoid(z)).astype(BF16)
    hv_o[...] = proj(C_HI, HG_WIDTH).astype(BF16)

    def gate_terms(c0, lb_ref, lf_o, kk_o):
        lb = lb_ref[...]
        sig = _sigmoid(proj(c0, HG_WIDTH))
        lf_o[...] = jnp.log(lb + (1.0 - lb) * sig)
        kk_o[...] = ((1.0 - lb) * (1.0 - sig)).astype(BF16)

    gate_terms(C_FF, lbf_ref, lff_o, kkf_o)
    gate_terms(C_FB, lbb_ref, lfb_o, kkb_o)

    def emit(val, o1, o4, o16):
        o1[...] = val.astype(BF16)
        for hp, tmp in enumerate((tmp_lo, tmp_hi)):
            cols = slice(hp * LANES, (hp + 1) * LANES)
            tmp[...] = val[:, cols]
            for dil, o in ((4, o4), (16, o16)):
                for r in range(dil):
                    o[r, :, cols] = tmp[pl.ds(r, ts // dil, stride=dil), :].astype(BF16)

    cos_t = cos_ref[...]
    sin_t = sin_ref[...]
    scale = LOG2E * HEAD64 ** -0.5
    qa = _rope(_headnorm64(proj(C_AQ, AT_WIDTH), aqw_ref[...]), cos_t, sin_t)
    emit(qa * scale, q1_o, q4_o, q16_o)
    emit(_rope(_headnorm64(proj(C_AK, AT_WIDTH), akw_ref[...]), cos_t, sin_t), k1_o, k4_o, k16_o)
    emit(proj(C_AV, AT_WIDTH), v1_o, v4_o, v16_o)
    mq_o[...] = (_headnorm64(proj(C_MQ, MEM_WIDTH), mqw_ref[...]) * scale).astype(BF16)
    for j in range(2):
        g = proj(C_GATE + j * 512, 512)
        g_o[:, j * 512:(j + 1) * 512] = (g * _sigmoid(g)).astype(BF16)


def _in_proj(x, nw, w_in, lbf, lbb, aqw, akw, mqw, cos_t, sin_t):
    B, S, _ = x.shape
    ts = PROJ_TILE
    per_att = ATT_TILE // ts
    row = lambda n: pl.BlockSpec((None, ts, n), lambda b, i: (b, i, 0))
    vec = lambda n: pl.BlockSpec((1, n), lambda b, i: (0, 0))
    tab = pl.BlockSpec((ts, LANES), lambda b, i: (i, 0))

    def grouped(dil):
        spec = pl.BlockSpec((None, None, dil, ts // dil, AT_WIDTH),
                            lambda b, i: (b, i // per_att, 0, i % per_att, 0))
        shape = jax.ShapeDtypeStruct((B, S // ATT_TILE, dil, ATT_TILE // dil, AT_WIDTH), BF16)
        return spec, shape

    outs = [(row(HG_WIDTH), jax.ShapeDtypeStruct((B, S, HG_WIDTH), dt))
            for dt in (BF16, BF16, F32, BF16, F32, BF16)]
    for _ in range(3):
        outs.append((row(AT_WIDTH), jax.ShapeDtypeStruct((B, S, AT_WIDTH), BF16)))
        outs.append(grouped(4))
        outs.append(grouped(16))
    outs.append((row(MEM_WIDTH), jax.ShapeDtypeStruct((B, S, MEM_WIDTH), BF16)))
    outs.append((row(1024), jax.ShapeDtypeStruct((B, S, 1024), BF16)))
    return pl.pallas_call(
        _in_proj_kernel,
        grid=(B, S // ts),
        in_specs=[row(D_MODEL), vec(D_MODEL),
                  pl.BlockSpec((D_MODEL, IN_WIDTH), lambda b, i: (0, 0)),
                  vec(HG_WIDTH), vec(HG_WIDTH), vec(AT_WIDTH), vec(AT_WIDTH), vec(MEM_WIDTH),
                  tab, tab],
        out_specs=[s for s, _ in outs],
        out_shape=[s for _, s in outs],
        scratch_shapes=[pltpu.VMEM((ts, LANES), F32)] * 2,
        compiler_params=_cparams("parallel", "parallel"),
        name="in_proj",
    )(x, nw, w_in, lbf, lbb, aqw, akw, mqw, cos_t, sin_t)


def _scan_kernel(q_ref, v_ref, kk_ref, lf_ref, o_ref, st_ref, *, reverse, ts):
    C = SCAN_CHUNK
    n_chunks = ts // C
    heads = range(HG_HEADS)
    chunks = range(n_chunks)
    nt_dims = (((1,), (1,)), ((), ()))
    tn_dims = (((0,), (0,)), ((), ()))
    row_sl = [slice(c * C, (c + 1) * C) for c in chunks]
    col_sl = [slice(hd * HG_D, (hd + 1) * HG_D) for hd in heads]

    @pl.when(pl.program_id(1) == 0)
    def _():
        st_ref[...] = jnp.zeros_like(st_ref)

    ri = lax.broadcasted_iota(jnp.int32, (C, C), 0)
    ci = lax.broadcasted_iota(jnp.int32, (C, C), 1)
    tri = (ci >= ri) if reverse else (ci <= ri)
    tri_bf = jnp.where(tri, 1.0, 0.0).astype(BF16)
    end_row = 0 if reverse else C - 1
    mid_row = C // 2 if reverse else C // 2 - 1

    b_all = []
    for c in chunks:
        lf = lf_ref[row_sl[c], :]
        lf_hi = lf.astype(BF16)
        lf_lo = (lf - lf_hi.astype(F32)).astype(BF16)
        b_all.append(jnp.dot(tri_bf, lf_hi, preferred_element_type=F32)
                     + jnp.dot(tri_bf, lf_lo, preferred_element_type=F32))
    q_inter, k_state, q_intra, k_intra, dec = [], [], [], [], []
    for c in chunks:
        b = b_all[c]
        q = q_ref[row_sl[c], :].astype(F32)
        k = kk_ref[row_sl[c], :].astype(F32)
        b_end = b[end_row:end_row + 1, :]
        b_mid = b[mid_row:mid_row + 1, :]
        dec.append(jnp.exp(b_end))
        q_inter.append((q * jnp.exp(b)).astype(BF16))
        k_state.append((k * jnp.exp(b_end - b)).astype(BF16))
        q_intra.append((q * jnp.exp(jnp.minimum(b - b_mid, EXP_CLAMP))).astype(BF16))
        k_intra.append((k * jnp.exp(jnp.minimum(b_mid - b, EXP_CLAMP))).astype(BF16))
    sc = [[lax.dot_general(q_intra[c][:, col_sl[hd]], k_intra[c][:, col_sl[hd]], nt_dims,
                           preferred_element_type=F32) for hd in heads] for c in chunks]
    upd = [[lax.dot_general(v_ref[row_sl[c], col_sl[hd]], k_state[c][:, col_sl[hd]], tn_dims,
                            preferred_element_type=F32) for hd in heads] for c in chunks]
    sc = [[jnp.where(tri, sc[c][hd], 0.0).astype(BF16) for hd in heads] for c in chunks]
    order = list(reversed(chunks)) if reverse else list(chunks)
    st_in = [[None] * HG_HEADS for _ in chunks]
    for hd in heads:
        st = st_ref[hd]
        for c in order:
            st_in[c][hd] = st.astype(BF16)
            st = st * dec[c][:, col_sl[hd]] + upd[c][hd]
        st_ref[hd] = st
    o_intra = [[jnp.dot(sc[c][hd], v_ref[row_sl[c], col_sl[hd]], preferred_element_type=F32)
                for hd in heads] for c in chunks]
    o_inter = [[lax.dot_general(q_inter[c][:, col_sl[hd]], st_in[c][hd], nt_dims,
                                preferred_element_type=F32) for hd in heads] for c in chunks]
    for c in chunks:
        for hd in heads:
            o_ref[row_sl[c], col_sl[hd]] = o_intra[c][hd] + o_inter[c][hd]


def _scan(q, v, kk, lf, reverse):
    B, S, _ = q.shape
    ts = SCAN_TILE
    nt = S // ts
    if reverse:
        imap = lambda b, i: (b, nt - 1 - i, 0)
    else:
        imap = lambda b, i: (b, i, 0)
    spec = pl.BlockSpec((None, ts, HG_WIDTH), imap)
    return pl.pallas_call(
        functools.partial(_scan_kernel, reverse=reverse, ts=ts),
        grid=(B, nt),
        in_specs=[spec, spec, spec, spec],
        out_specs=spec,
        out_shape=jax.ShapeDtypeStruct((B, S, HG_WIDTH), F32),
        scratch_shapes=[pltpu.VMEM((HG_HEADS, HG_D, HG_D), F32)],
        compiler_params=_cparams("parallel", "arbitrary"),
        name="scan_bwd" if reverse else "scan_fwd",
    )(q, v, kk, lf)


def _mem_kv_kernel(m_ref, nw_ref, w_ref, kw_ref, mk_o, mva_o, mvb_o):
    x = m_ref[...]
    ms = jnp.mean(x * x, axis=-1, keepdims=True)
    h = (x * lax.rsqrt(ms + NORM_EPS) * nw_ref[...]).astype(BF16)
    kv = jnp.dot(h, w_ref[...], preferred_element_type=F32)
    mk_o[...] = _headnorm64(kv[:, :MEM_WIDTH], kw_ref[...]).astype(BF16)
    lo_mask = (lax.broadcasted_iota(jnp.int32, (1, MEM_WIDTH), 1) % LANES) < HEAD64
    mv = kv[:, MEM_WIDTH:]
    mva_o[...] = jnp.where(lo_mask, mv, 1.0).astype(BF16)
    mvb_o[...] = jnp.where(lo_mask, 1.0, mv).astype(BF16)


def _mem_kv(mem, nw, wkv, kw):
    B, M, _ = mem.shape
    out = pl.BlockSpec((None, M, MEM_WIDTH), lambda b: (b, 0, 0))
    return pl.pallas_call(
        _mem_kv_kernel,
        grid=(B,),
        in_specs=[pl.BlockSpec((None, M, D_MODEL), lambda b: (b, 0, 0)),
                  pl.BlockSpec((1, D_MODEL), lambda b: (0, 0)),
                  pl.BlockSpec((D_MODEL, 2 * MEM_WIDTH), lambda b: (0, 0)),
                  pl.BlockSpec((1, MEM_WIDTH), lambda b: (0, 0))],
        out_specs=[out, out, out],
        out_shape=[jax.ShapeDtypeStruct((B, M, MEM_WIDTH), BF16)] * 3,
        compiler_params=_cparams("parallel"),
        name="mem_kv",
    )(mem, nw, wkv, kw)


def _attn_kernel(negm_ref, q1_ref, q4_ref, q16_ref,
                 k1p, k1m, k1n, k4p, k4m, k4n, k16p, k16m, k16n,
                 v1p, v1m, v1n, v4p, v4m, v4n, v16p, v16m, v16n,
                 o_ref,
                 kx1, kx4, kx16, va1, va4, va16, vb1, vb4, vb16,
                 num_lo, num_hi, den_lo, den_hi, bias_s, *, nt):
    num_s = (num_lo, num_hi)
    den_s = (den_lo, den_hi)
    i = pl.program_id(1)
    first = i == 0
    last = i == nt - 1
    lo_mask = _lane_iota(AT_WIDTH) % LANES < HEAD64

    qi = lax.broadcasted_iota(jnp.int32, (QBLK, QBLK + 2 * BAND), 0)
    ke = lax.broadcasted_iota(jnp.int32, (QBLK, QBLK + 2 * BAND), 1)
    delta = ke - BAND - qi
    bias_s[...] = jnp.where((delta <= BAND) & (delta >= -BAND), negm_ref[...], MASK_VALUE)

    def fill(kx, va, vb, kp, km, kn, vp, vm, vn):
        nres, rows = km.shape[0], km.shape[1]
        for r in range(nres):
            for dst, src, edge in ((slice(0, BAND), (kp, vp), first),
                                   (slice(BAND, BAND + rows), (km, vm), None),
                                   (slice(BAND + rows, rows + 2 * BAND), (kn, vn), last)):
                k = src[0][r]
                v = src[1][r]
                v_a = jnp.where(lo_mask, v, jnp.ones_like(v))
                v_b = jnp.where(lo_mask, jnp.ones_like(v), v)
                if edge is not None:
                    zero = jnp.zeros_like(k)
                    k = jnp.where(edge, zero, k)
                    v_a = jnp.where(edge, zero, v_a)
                    v_b = jnp.where(edge, zero, v_b)
                kx[r, dst, :] = k
                va[r, dst, :] = v_a
                vb[r, dst, :] = v_b

    fill(kx1, va1, vb1, k1p, k1m, k1n, v1p, v1m, v1n)
    fill(kx4, va4, vb4, k4p, k4m, k4n, v4p, v4m, v4n)
    fill(kx16, va16, vb16, k16p, k16m, k16n, v16p, v16m, v16n)

    def pattern(dil, q_ref, kx, va, vb):
        blocks_per_res = ATT_TILE // dil // QBLK

        def body(grp, carry):
            pairs, dests = [], []
            for u in range(ATT_GROUP):
                it = grp * ATT_GROUP + u
                res = it // blocks_per_res
                a = pl.multiple_of((it % blocks_per_res) * QBLK, QBLK)
                out_rows = pl.ds(a, QBLK) if dil == 1 else pl.ds(a * dil + res, QBLK, stride=dil)
                win = pl.ds(a, QBLK + 2 * BAND)
                for hp in range(AT_WIDTH // LANES):
                    cols = slice(hp * LANES, (hp + 1) * LANES)
                    pairs.append((q_ref[res, pl.ds(a, QBLK), cols], kx[res, win, cols],
                                  va[res, win, cols], vb[res, win, cols]))
                    dests.append((hp, out_rows))
            for (hp, out_rows), (num, den) in zip(dests, _pairs_attend(pairs, bias_s[...])):
                if dil == 1:
                    num_s[hp][out_rows, :] = num
                    den_s[hp][out_rows, :] = den
                else:
                    num_s[hp][out_rows, :] += num
                    den_s[hp][out_rows, :] += den
            return carry

        lax.fori_loop(0, dil * blocks_per_res // ATT_GROUP, body, 0)

    pattern(1, q1_ref, kx1, va1, vb1)
    pattern(4, q4_ref, kx4, va4, vb4)
    pattern(16, q16_ref, kx16, va16, vb16)
    for hp in range(AT_WIDTH // LANES):
        o_ref[:, hp * LANES:(hp + 1) * LANES] = (num_s[hp][...] / den_s[hp][...]).astype(BF16)


def _attention(negm, q, k, v):
    B, S, _ = q[0].shape
    nt = S // ATT_TILE

    def view(t, dil):
        return t.reshape(B, nt, dil, ATT_TILE // dil, AT_WIDTH)

    def main(dil):
        return pl.BlockSpec((None, None, dil, ATT_TILE // dil, AT_WIDTH),
                            lambda b, i: (b, i, 0, 0, 0))

    def prev(dil):
        lastblk = ATT_TILE // dil // BAND - 1
        return pl.BlockSpec((None, None, dil, BAND, AT_WIDTH),
                            lambda b, i: (b, jnp.maximum(i - 1, 0), 0, lastblk, 0))

    def nxt(dil):
        return pl.BlockSpec((None, None, dil, BAND, AT_WIDTH),
                            lambda b, i: (b, jnp.minimum(i + 1, nt - 1), 0, 0, 0))

    qs = [view(t, d) for t, d in zip(q, DILATIONS)]
    ks = [view(t, d) for t, d in zip(k, DILATIONS)]
    vs = [view(t, d) for t, d in zip(v, DILATIONS)]
    halo_specs, halo_args = [], []
    for ts_ in (ks, vs):
        for t, d in zip(ts_, DILATIONS):
            halo_specs += [prev(d), main(d), nxt(d)]
            halo_args += [t, t, t]
    ext = lambda d, dt: pltpu.VMEM((d, ATT_TILE // d + 2 * BAND, AT_WIDTH), dt)
    return pl.pallas_call(
        functools.partial(_attn_kernel, nt=nt),
        grid=(B, nt),
        in_specs=[pl.BlockSpec((1, QBLK + 2 * BAND), lambda b, i: (0, 0))]
                 + [main(d) for d in DILATIONS] + halo_specs,
        out_specs=pl.BlockSpec((None, ATT_TILE, AT_WIDTH), lambda b, i: (b, i, 0)),
        out_shape=jax.ShapeDtypeStruct((B, S, AT_WIDTH), BF16),
        scratch_shapes=[ext(d, BF16) for d in DILATIONS] * 3
                       + [pltpu.VMEM((ATT_TILE, LANES), F32)] * 4
                       + [pltpu.VMEM((QBLK, QBLK + 2 * BAND), F32)],
        compiler_params=_cparams("parallel", "parallel"),
        name="dilated_attn",
    )(negm, *qs, *halo_args)


def _out_kernel(x_ref, of_ref, ob_ref, g_ref, oat_ref, mq_ref, mk_ref, mva_ref, mvb_ref,
                negm_ref, onw_ref, w_ref, y_ref):
    g = g_ref[...].astype(F32)
    parts = []
    for hd in range(HG_HEADS):
        cols = slice(hd * HG_D, (hd + 1) * HG_D)
        t = of_ref[:, cols] + ob_ref[:, cols]
        ms = jnp.mean(t * t, axis=-1, keepdims=True)
        parts.append(t * lax.rsqrt(ms + NORM_EPS) * onw_ref[...] * g[:, cols])
    parts.append(oat_ref[...].astype(F32) * g[:, HG_WIDTH:HG_WIDTH + AT_WIDTH])
    for hp in range(MEM_WIDTH // LANES):
        cols = slice(hp * LANES, (hp + 1) * LANES)
        num, den = _pair_attend(mq_ref[:, cols], mk_ref[:, cols], mva_ref[:, cols],
                                mvb_ref[:, cols], negm_ref[...])
        gc = slice(HG_WIDTH + AT_WIDTH + hp * LANES, HG_WIDTH + AT_WIDTH + (hp + 1) * LANES)
        parts.append(num / den * g[:, gc])
    mixed = jnp.concatenate(parts, axis=-1).astype(BF16)
    y_ref[...] = x_ref[...] + jnp.dot(mixed, w_ref[...], preferred_element_type=F32)


def _out_stage(x, o_f, o_b, g, o_at, mq, mk, mva, mvb, negm, onw, w_out):
    B, S, _ = x.shape
    ts = PROJ_TILE
    row = lambda n: pl.BlockSpec((None, ts, n), lambda b, i: (b, i, 0))
    memspec = pl.BlockSpec((None, MEM_TOKENS, MEM_WIDTH), lambda b, i: (b, 0, 0))
    return pl.pallas_call(
        _out_kernel,
        grid=(B, S // ts),
        in_specs=[row(D_MODEL), row(HG_WIDTH), row(HG_WIDTH), row(1024), row(AT_WIDTH),
                  row(MEM_WIDTH), memspec, memspec, memspec,
                  pl.BlockSpec((1, MEM_TOKENS), lambda b, i: (0, 0)),
                  pl.BlockSpec((1, HG_D), lambda b, i: (0, 0)),
                  pl.BlockSpec((D_MODEL, D_MODEL), lambda b, i: (0, 0))],
        out_specs=row(D_MODEL),
        out_shape=jax.ShapeDtypeStruct((B, S, D_MODEL), F32),
        compiler_params=_cparams("parallel", "parallel"),
        name="out_stage",
    )(x, o_f, o_b, g, o_at, mq, mk, mva, mvb, negm, onw, w_out)


def _rope_tables(S):
    half = ROPE_DIM // 2
    pos = jnp.arange(S, dtype=F32)
    inv_freq = ROPE_THETA ** (-jnp.arange(half, dtype=F32) * 2.0 / ROPE_DIM)
    ang = pos[:, None] * inv_freq[None, :]
    cos, sin = jnp.cos(ang), jnp.sin(ang)
    pad = HEAD64 - ROPE_DIM
    cos64 = jnp.concatenate([cos, cos, jnp.ones((S, pad), F32)], axis=-1)
    sin64 = jnp.concatenate([-sin, sin, jnp.zeros((S, pad), F32)], axis=-1)
    return jnp.tile(cos64, (1, 2)), jnp.tile(sin64, (1, 2))


def _lower_bounds(p):
    sm = jax.nn.softmax(p.astype(F32), axis=0)
    return jnp.cumsum(sm, axis=0) - sm[0:1]


def _tile4(w):
    return jnp.tile(w.astype(F32), (1, 4))


def _neg_stabiliser(wq, wk, width):
    bound = (HEAD64 ** 0.5 * LOG2E) * jnp.max(jnp.abs(wq), axis=-1) * jnp.max(jnp.abs(wk), axis=-1)
    bound = jnp.minimum(bound.astype(F32), EXP2_RANGE)
    return jnp.broadcast_to(-bound[:, None, None], (DEPTH, 1, width))


def _trunk(x, mem, prm):
    B, S, _ = x.shape
    assert S % ATT_TILE == 0 and S % PROJ_TILE == 0 and S % SCAN_TILE == 0
    cos_t, sin_t = _rope_tables(S)
    for l in range(DEPTH):
        p = {k: v[l] for k, v in prm.items()}
        (hq, hv, lff, kkf, lfb, kkb, q1, q4, q16, k1, k4, k16, v1, v4, v16, mq, g) = _in_proj(
            x, p["norm_w"], p["w_in"], p["lbf"], p["lbb"], p["aqw"], p["akw"], p["mqw"],
            cos_t, sin_t)
        o_f = _scan(hq, hv, kkf, lff, False)
        o_b = _scan(hq, hv, kkb, lfb, True)
        mk, mva, mvb = _mem_kv(mem, p["mem_norm_w"], p["mem_wkv"], p["mkw"])
        o_at = _attention(p["negm_at"], (q1, q4, q16), (k1, k4, k16), (v1, v4, v16))
        x = _out_stage(x, o_f, o_b, g, o_at, mq, mk, mva, mvb, p["negm_mem"], p["onw"],
                       p["w_out"])
    return x


def kernel(x_prompt, x_sample, mem_prompt, mem_sample, norm_w, w_in, hgrn_lb_fwd, hgrn_lb_bwd,
           hgrn_onorm_w, attn_qnorm_w, attn_knorm_w, mem_norm_w, mem_wkv, mem_qnorm_w,
           mem_knorm_w, w_out):
    prm = {
        "norm_w": norm_w.astype(F32)[:, None, :],
        "w_in": w_in.astype(BF16),
        "lbf": _lower_bounds(hgrn_lb_fwd)[:, None, :],
        "lbb": _lower_bounds(hgrn_lb_bwd)[:, None, :],
        "onw": hgrn_onorm_w.astype(F32)[:, None, :],
        "aqw": _tile4(attn_qnorm_w)[:, None, :],
        "akw": _tile4(attn_knorm_w)[:, None, :],
        "mem_norm_w": mem_norm_w.astype(F32)[:, None, :],
        "mem_wkv": mem_wkv.astype(BF16),
        "mqw": _tile4(mem_qnorm_w)[:, None, :],
        "mkw": _tile4(mem_knorm_w)[:, None, :],
        "w_out": w_out.astype(BF16),
        "negm_at": _neg_stabiliser(attn_qnorm_w, attn_knorm_w, QBLK + 2 * BAND),
        "negm_mem": _neg_stabiliser(mem_qnorm_w, mem_knorm_w, MEM_TOKENS),
    }
    y_prompt = _trunk(x_prompt, mem_prompt, prm)
    y_sample = _trunk(x_sample, mem_sample, prm)
    return (y_prompt, y_sample)
```

```python
import functools

import jax
import jax.numpy as jnp
from jax import lax
from jax.experimental import pallas as pl
from jax.experimental.pallas import tpu as pltpu

F32 = jnp.float32
BF16 = jnp.bfloat16

D_MODEL = 1024
DEPTH = 4
HG_HEADS = 4
HG_D = 128
HG_WIDTH = HG_HEADS * HG_D
AT_WIDTH = 256
HEAD64 = 64
MEM_WIDTH = 256
MEM_TOKENS = 256
IN_WIDTH = 4096
DILATIONS = (1, 4, 16)
BAND = 64
ROPE_THETA = 500000.0
ROPE_DIM = 16
NORM_EPS = 1e-6
MASK_VALUE = -1e30
LANES = 128

C_HQ, C_FF, C_FB, C_HI = 0, 512, 1024, 1536
C_AQ, C_AK, C_AV, C_MQ = 2048, 2304, 2560, 2816
C_GATE = 3072

LOG2E = 1.4426950408889634
EXP2_RANGE = 63.0
PROJ_TILE = 512
ATT_TILE = 2048
QBLK = 128
ATT_GROUP = 4
SCAN_CHUNK = 64
EXP_CLAMP = 80.0
VMEM_LIMIT = 56 * 1024 * 1024


def _cparams(*sem):
    return pltpu.CompilerParams(dimension_semantics=sem, vmem_limit_bytes=VMEM_LIMIT)


def _sigmoid(z):
    return 1.0 / (1.0 + jnp.exp(-z))


def _lane_iota(n=LANES):
    return lax.broadcasted_iota(jnp.int32, (1, n), 1)


def _headnorm64(t, w):
    lo_mask = _lane_iota() < HEAD64
    parts = []
    for hp in range(t.shape[1] // LANES):
        th = t[:, hp * LANES:(hp + 1) * LANES]
        t2 = th * th
        lo = jnp.sum(jnp.where(lo_mask, t2, 0.0), axis=-1, keepdims=True)
        hi = jnp.sum(jnp.where(lo_mask, 0.0, t2), axis=-1, keepdims=True)
        ms = jnp.where(lo_mask, lo, hi) * (1.0 / HEAD64)
        parts.append(th * lax.rsqrt(ms + NORM_EPS))
    return jnp.concatenate(parts, axis=-1) * w


def _rope(t, cos_t, sin_t):
    lane = _lane_iota() % HEAD64
    parts = []
    for hp in range(t.shape[1] // LANES):
        th = t[:, hp * LANES:(hp + 1) * LANES]
        partner = jnp.where(lane < ROPE_DIM // 2,
                            pltpu.roll(th, LANES - ROPE_DIM // 2, 1),
                            pltpu.roll(th, ROPE_DIM // 2, 1))
        parts.append(th * cos_t + partner * sin_t)
    return jnp.concatenate(parts, axis=-1)


def _pair_attend(q2, k2, va, vb, bias):
    return _pairs_attend([(q2, k2, va, vb)], bias)[0]


def _pairs_attend(pairs, bias):
    lo_mask = _lane_iota() < HEAD64
    nt = (((1,), (1,)), ((), ()))
    s = []
    for q2, k2, _, _ in pairs:
        zero = jnp.zeros_like(q2)
        s.append((lax.dot_general(jnp.where(lo_mask, q2, zero), k2, nt, preferred_element_type=F32),
                  lax.dot_general(jnp.where(lo_mask, zero, q2), k2, nt, preferred_element_type=F32)))
    p = [(jnp.exp2(s_lo + bias).astype(BF16), jnp.exp2(s_hi + bias).astype(BF16)) for s_lo, s_hi in s]
    o = [(jnp.dot(p_lo, va, preferred_element_type=F32), jnp.dot(p_hi, vb, preferred_element_type=F32))
         for (p_lo, p_hi), (_, _, va, vb) in zip(p, pairs)]
    return [(jnp.where(lo_mask, o_lo, o_hi), pltpu.roll(jnp.where(lo_mask, o_hi, o_lo), HEAD64, 1))
            for o_lo, o_hi in o]


def _front_kernel(x_ref, nw_ref, w_ref, lbf_ref, lbb_ref, aqw_ref, akw_ref, mqw_ref,
                  cos_ref, sin_ref,
                  osum_o, qtb_o, lst_o, dtile_o,
                  q1_o, q4_o, q16_o, k1_o, k4_o, k16_o, v1_o, v4_o, v16_o, mq_o, g_o,
                  st_ref, tmp_lo, tmp_hi):
    ts = x_ref.shape[0]
    C = SCAN_CHUNK
    chunks = range(ts // C)
    heads = range(HG_HEADS)
    nt_dims = (((1,), (1,)), ((), ()))
    tn_dims = (((0,), (0,)), ((), ()))
    row_sl = [slice(c * C, (c + 1) * C) for c in chunks]
    col_sl = [slice(hd * HG_D, (hd + 1) * HG_D) for hd in heads]

    @pl.when(pl.program_id(1) == 0)
    def _():
        st_ref[...] = jnp.zeros_like(st_ref)

    x = x_ref[...]
    ms = jnp.mean(x * x, axis=-1, keepdims=True)
    h = (x * lax.rsqrt(ms + NORM_EPS) * nw_ref[...]).astype(BF16)

    def proj(c0, n):
        return jnp.dot(h, w_ref[:, c0:c0 + n], preferred_element_type=F32)

    def gate_terms(z, lb_ref):
        lb = lb_ref[...]
        sig = _sigmoid(z)
        return jnp.log(lb + (1.0 - lb) * sig), (1.0 - lb) * (1.0 - sig)

    z_q = proj(C_HQ, HG_WIDTH)
    z_i = proj(C_HI, HG_WIDTH)
    q = z_q * _sigmoid(z_q)
    z_f = proj(C_FF, HG_WIDTH)
    v_bf = z_i.astype(BF16)
    z_b = proj(C_FB, HG_WIDTH)
    lf_f, kk_f = gate_terms(z_f, lbf_ref)
    z_aq = proj(C_AQ, AT_WIDTH)
    z_ak = proj(C_AK, AT_WIDTH)
    lf_b, kk_b = gate_terms(z_b, lbb_ref)

    ri = lax.broadcasted_iota(jnp.int32, (C, C), 0)
    ci = lax.broadcasted_iota(jnp.int32, (C, C), 1)
    tri = [ci <= ri, ci >= ri]
    tri_bf = [jnp.where(t, 1.0, 0.0).astype(BF16) for t in tri]
    end_row = [C - 1, 0]
    mid_row = [C // 2 - 1, C // 2]
    lf_d, kk_d = [lf_f, lf_b], [kk_f, kk_b]
    dirs = range(2)

    b_all = [[None] * len(chunks) for _ in dirs]
    for d in dirs:
        for c in chunks:
            lf = lf_d[d][row_sl[c], :]
            lf_hi = lf.astype(BF16)
            lf_lo = (lf - lf_hi.astype(F32)).astype(BF16)
            b_all[d][c] = (jnp.dot(tri_bf[d], lf_hi, preferred_element_type=F32)
                           + jnp.dot(tri_bf[d], lf_lo, preferred_element_type=F32))
    z_av = proj(C_AV, AT_WIDTH)
    z_mq = proj(C_MQ, MEM_WIDTH)
    z_g0 = proj(C_GATE, 512)

    q_inter = [[None] * len(chunks) for _ in dirs]
    k_state = [[None] * len(chunks) for _ in dirs]
    q_intra = [[None] * len(chunks) for _ in dirs]
    k_intra = [[None] * len(chunks) for _ in dirs]
    dec = [[None] * len(chunks) for _ in dirs]
    for d in dirs:
        for c in chunks:
            b = b_all[d][c]
            qc = q[row_sl[c], :]
            kc = kk_d[d][row_sl[c], :]
            b_end = b[end_row[d]:end_row[d] + 1, :]
            b_mid = b[mid_row[d]:mid_row[d] + 1, :]
            dec[d][c] = jnp.exp(b_end)
            q_inter[d][c] = (qc * jnp.exp(b)).astype(BF16)
            k_state[d][c] = (kc * jnp.exp(b_end - b)).astype(BF16)
            q_intra[d][c] = (qc * jnp.exp(jnp.minimum(b - b_mid, EXP_CLAMP))).astype(BF16)
            k_intra[d][c] = (kc * jnp.exp(jnp.minimum(b_mid - b, EXP_CLAMP))).astype(BF16)
    off = jnp.zeros((1, HG_WIDTH), F32)
    for c in reversed(chunks):
        b = b_all[1][c]
        qtb_o[row_sl[c], :] = (q[row_sl[c], :] * jnp.exp(b + off)).astype(BF16)
        off = off + b[end_row[1]:end_row[1] + 1, :]
    dtile_o[...] = jnp.exp(off)
    z_g1 = proj(C_GATE + 512, 512)

    sc = [[[lax.dot_general(q_intra[d][c][:, col_sl[hd]], k_intra[d][c][:, col_sl[hd]], nt_dims,
                            preferred_element_type=F32) for hd in heads] for c in chunks]
          for d in dirs]
    upd = [[[lax.dot_general(v_bf[row_sl[c], col_sl[hd]], k_state[d][c][:, col_sl[hd]], tn_dims,
                             preferred_element_type=F32) for hd in heads] for c in chunks]
           for d in dirs]
    g_o[:, 0:512] = (z_g0 * _sigmoid(z_g0)).astype(BF16)
    sc = [[[jnp.where(tri[d], sc[d][c][hd], 0.0).astype(BF16) for hd in heads] for c in chunks]
          for d in dirs]

    st_in = [[[None] * HG_HEADS for _ in chunks] for _ in dirs]
    for hd in heads:
        st = st_ref[hd]
        for c in chunks:
            st_in[0][c][hd] = st.astype(BF16)
            st = st * dec[0][c][:, col_sl[hd]] + upd[0][c][hd]
        st_ref[hd] = st
        st = None
        for c in reversed(chunks):
            if st is None:
                st = upd[1][c][hd]
            else:
                st_in[1][c][hd] = st.astype(BF16)
                st = st * dec[1][c][:, col_sl[hd]] + upd[1][c][hd]
        lst_o[hd] = st
    o_parts = [[[] for _ in heads] for _ in chunks]
    for d in dirs:
        for c in chunks:
            for hd in heads:
                o_parts[c][hd].append(jnp.dot(sc[d][c][hd], v_bf[row_sl[c], col_sl[hd]],
                                              preferred_element_type=F32))
                if st_in[d][c][hd] is not None:
                    o_parts[c][hd].append(lax.dot_general(q_inter[d][c][:, col_sl[hd]],
                                                          st_in[d][c][hd], nt_dims,
                                                          preferred_element_type=F32))
    g_o[:, 512:1024] = (z_g1 * _sigmoid(z_g1)).astype(BF16)

    def emit(val, o1, o4, o16):
        o1[...] = val.astype(BF16)
        for hp, tmp in enumerate((tmp_lo, tmp_hi)):
            cols = slice(hp * LANES, (hp + 1) * LANES)
            tmp[...] = val[:, cols]
            for dil, o in ((4, o4), (16, o16)):
                for r in range(dil):
                    o[r, :, cols] = tmp[pl.ds(r, ts // dil, stride=dil), :].astype(BF16)

    cos_t = cos_ref[...]
    sin_t = sin_ref[...]
    scale = LOG2E * HEAD64 ** -0.5
    emit(_rope(_headnorm64(z_aq, aqw_ref[...]), cos_t, sin_t) * scale, q1_o, q4_o, q16_o)
    emit(_rope(_headnorm64(z_ak, akw_ref[...]), cos_t, sin_t), k1_o, k4_o, k16_o)
    emit(z_av, v1_o, v4_o, v16_o)
    mq_o[...] = (_headnorm64(z_mq, mqw_ref[...]) * scale).astype(BF16)
    for c in chunks:
        for hd in heads:
            o_sum = o_parts[c][hd][0]
            for part in o_parts[c][hd][1:]:
                o_sum = o_sum + part
            osum_o[row_sl[c], col_sl[hd]] = o_sum.astype(BF16)


def _front(x, nw, w_in, lbf, lbb, aqw, akw, mqw, cos_t, sin_t):
    B, S, _ = x.shape
    ts = PROJ_TILE
    nt = S // ts
    per_att = ATT_TILE // ts
    row = lambda n: pl.BlockSpec((None, ts, n), lambda b, i: (b, i, 0))
    vec = lambda n: pl.BlockSpec((1, n), lambda b, i: (0, 0))
    tab = pl.BlockSpec((ts, LANES), lambda b, i: (i, 0))

    def grouped(dil):
        spec = pl.BlockSpec((None, None, dil, ts // dil, AT_WIDTH),
                            lambda b, i: (b, i // per_att, 0, i % per_att, 0))
        shape = jax.ShapeDtypeStruct((B, S // ATT_TILE, dil, ATT_TILE // dil, AT_WIDTH), BF16)
        return spec, shape

    outs = [(row(HG_WIDTH), jax.ShapeDtypeStruct((B, S, HG_WIDTH), BF16)),
            (row(HG_WIDTH), jax.ShapeDtypeStruct((B, S, HG_WIDTH), BF16)),
            (pl.BlockSpec((None, None, HG_HEADS, HG_D, HG_D), lambda b, i: (b, i, 0, 0, 0)),
             jax.ShapeDtypeStruct((B, nt, HG_HEADS, HG_D, HG_D), F32)),
            (pl.BlockSpec((None, None, 1, HG_WIDTH), lambda b, i: (b, i, 0, 0)),
             jax.ShapeDtypeStruct((B, nt, 1, HG_WIDTH), F32))]
    for _ in range(3):
        outs.append((row(AT_WIDTH), jax.ShapeDtypeStruct((B, S, AT_WIDTH), BF16)))
        outs.append(grouped(4))
        outs.append(grouped(16))
    outs.append((row(MEM_WIDTH), jax.ShapeDtypeStruct((B, S, MEM_WIDTH), BF16)))
    outs.append((row(1024), jax.ShapeDtypeStruct((B, S, 1024), BF16)))
    return pl.pallas_call(
        _front_kernel,
        grid=(B, nt),
        in_specs=[row(D_MODEL), vec(D_MODEL),
                  pl.BlockSpec((D_MODEL, IN_WIDTH), lambda b, i: (0, 0)),
                  vec(HG_WIDTH), vec(HG_WIDTH), vec(AT_WIDTH), vec(AT_WIDTH), vec(MEM_WIDTH),
                  tab, tab],
        out_specs=[s for s, _ in outs],
        out_shape=[s for _, s in outs],
        scratch_shapes=[pltpu.VMEM((HG_HEADS, HG_D, HG_D), F32)]
                       + [pltpu.VMEM((ts, LANES), F32)] * 2,
        compiler_params=_cparams("parallel", "arbitrary"),
        name="front",
    )(x, nw, w_in, lbf, lbb, aqw, akw, mqw, cos_t, sin_t)


def _mem_kv_kernel(m_ref, nw_ref, w_ref, kw_ref, mk_o, mva_o, mvb_o):
    x = m_ref[...]
    ms = jnp.mean(x * x, axis=-1, keepdims=True)
    h = (x * lax.rsqrt(ms + NORM_EPS) * nw_ref[...]).astype(BF16)
    kv = jnp.dot(h, w_ref[...], preferred_element_type=F32)
    mk_o[...] = _headnorm64(kv[:, :MEM_WIDTH], kw_ref[...]).astype(BF16)
    lo_mask = (lax.broadcasted_iota(jnp.int32, (1, MEM_WIDTH), 1) % LANES) < HEAD64
    mv = kv[:, MEM_WIDTH:]
    mva_o[...] = jnp.where(lo_mask, mv, 1.0).astype(BF16)
    mvb_o[...] = jnp.where(lo_mask, 1.0, mv).astype(BF16)


def _mem_kv(mem, nw, wkv, kw):
    B, M, _ = mem.shape
    out = pl.BlockSpec((None, M, MEM_WIDTH), lambda b: (b, 0, 0))
    return pl.pallas_call(
        _mem_kv_kernel,
        grid=(B,),
        in_specs=[pl.BlockSpec((None, M, D_MODEL), lambda b: (b, 0, 0)),
                  pl.BlockSpec((1, D_MODEL), lambda b: (0, 0)),
                  pl.BlockSpec((D_MODEL, 2 * MEM_WIDTH), lambda b: (0, 0)),
                  pl.BlockSpec((1, MEM_WIDTH), lambda b: (0, 0))],
        out_specs=[out, out, out],
        out_shape=[jax.ShapeDtypeStruct((B, M, MEM_WIDTH), BF16)] * 3,
        compiler_params=_cparams("parallel"),
        name="mem_kv",
    )(mem, nw, wkv, kw)


def _attn_kernel(negm_ref, q1_ref, q4_ref, q16_ref,
                 k1p, k1m, k1n, k4p, k4m, k4n, k16p, k16m, k16n,
                 v1p, v1m, v1n, v4p, v4m, v4n, v16p, v16m, v16n,
                 o_ref,
                 kx1, kx4, kx16, va1, va4, va16, vb1, vb4, vb16,
                 num_lo, num_hi, den_lo, den_hi, bias_s, *, nt):
    num_s = (num_lo, num_hi)
    den_s = (den_lo, den_hi)
    i = pl.program_id(1)
    first = i == 0
    last = i == nt - 1
    lo_mask = _lane_iota(AT_WIDTH) % LANES < HEAD64

    qi = lax.broadcasted_iota(jnp.int32, (QBLK, QBLK + 2 * BAND), 0)
    ke = lax.broadcasted_iota(jnp.int32, (QBLK, QBLK + 2 * BAND), 1)
    delta = ke - BAND - qi
    bias_s[...] = jnp.where((delta <= BAND) & (delta >= -BAND), negm_ref[...], MASK_VALUE)

    def fill(kx, va, vb, kp, km, kn, vp, vm, vn):
        nres, rows = km.shape[0], km.shape[1]
        for r in range(nres):
            for dst, src, edge in ((slice(0, BAND), (kp, vp), first),
                                   (slice(BAND, BAND + rows), (km, vm), None),
                                   (slice(BAND + rows, rows + 2 * BAND), (kn, vn), last)):
                k = src[0][r]
                v = src[1][r]
                v_a = jnp.where(lo_mask, v, jnp.ones_like(v))
                v_b = jnp.where(lo_mask, jnp.ones_like(v), v)
                if edge is not None:
                    zero = jnp.zeros_like(k)
                    k = jnp.where(edge, zero, k)
                    v_a = jnp.where(edge, zero, v_a)
                    v_b = jnp.where(edge, zero, v_b)
                kx[r, dst, :] = k
                va[r, dst, :] = v_a
                vb[r, dst, :] = v_b

    fill(kx1, va1, vb1, k1p, k1m, k1n, v1p, v1m, v1n)
    fill(kx4, va4, vb4, k4p, k4m, k4n, v4p, v4m, v4n)
    fill(kx16, va16, vb16, k16p, k16m, k16n, v16p, v16m, v16n)

    def pattern(dil, q_ref, kx, va, vb):
        blocks_per_res = ATT_TILE // dil // QBLK

        def body(grp, carry):
            pairs, dests = [], []
            for u in range(ATT_GROUP):
                it = grp * ATT_GROUP + u
                res = it // blocks_per_res
                a = pl.multiple_of((it % blocks_per_res) * QBLK, QBLK)
                out_rows = pl.ds(a, QBLK) if dil == 1 else pl.ds(a * dil + res, QBLK, stride=dil)
                win = pl.ds(a, QBLK + 2 * BAND)
                for hp in range(AT_WIDTH // LANES):
                    cols = slice(hp * LANES, (hp + 1) * LANES)
                    pairs.append((q_ref[res, pl.ds(a, QBLK), cols], kx[res, win, cols],
                                  va[res, win, cols], vb[res, win, cols]))
                    dests.append((hp, out_rows))
            for (hp, out_rows), (num, den) in zip(dests, _pairs_attend(pairs, bias_s[...])):
                if dil == 1:
                    num_s[hp][out_rows, :] = num
                    den_s[hp][out_rows, :] = den
                else:
                    num_s[hp][out_rows, :] += num
                    den_s[hp][out_rows, :] += den
            return carry

        lax.fori_loop(0, dil * blocks_per_res // ATT_GROUP, body, 0)

    pattern(1, q1_ref, kx1, va1, vb1)
    pattern(4, q4_ref, kx4, va4, vb4)
    pattern(16, q16_ref, kx16, va16, vb16)
    for hp in range(AT_WIDTH // LANES):
        o_ref[:, hp * LANES:(hp + 1) * LANES] = (num_s[hp][...] / den_s[hp][...]).astype(BF16)


def _attention(negm, q, k, v):
    B, S, _ = q[0].shape
    nt = S // ATT_TILE

    def view(t, dil):
        return t.reshape(B, nt, dil, ATT_TILE // dil, AT_WIDTH)

    def main(dil):
        return pl.BlockSpec((None, None, dil, ATT_TILE // dil, AT_WIDTH),
                            lambda b, i: (b, i, 0, 0, 0))

    def prev(dil):
        lastblk = ATT_TILE // dil // BAND - 1
        return pl.BlockSpec((None, None, dil, BAND, AT_WIDTH),
                            lambda b, i: (b, jnp.maximum(i - 1, 0), 0, lastblk, 0))

    def nxt(dil):
        return pl.BlockSpec((None, None, dil, BAND, AT_WIDTH),
                            lambda b, i: (b, jnp.minimum(i + 1, nt - 1), 0, 0, 0))

    qs = [view(t, d) for t, d in zip(q, DILATIONS)]
    ks = [view(t, d) for t, d in zip(k, DILATIONS)]
    vs = [view(t, d) for t, d in zip(v, DILATIONS)]
    halo_specs, halo_args = [], []
    for ts_ in (ks, vs):
        for t, d in zip(ts_, DILATIONS):
            halo_specs += [prev(d), main(d), nxt(d)]
            halo_args += [t, t, t]
    ext = lambda d, dt: pltpu.VMEM((d, ATT_TILE // d + 2 * BAND, AT_WIDTH), dt)
    return pl.pallas_call(
        functools.partial(_attn_kernel, nt=nt),
        grid=(B, nt),
        in_specs=[pl.BlockSpec((1, QBLK + 2 * BAND), lambda b, i: (0, 0))]
                 + [main(d) for d in DILATIONS] + halo_specs,
        out_specs=pl.BlockSpec((None, ATT_TILE, AT_WIDTH), lambda b, i: (b, i, 0)),
        out_shape=jax.ShapeDtypeStruct((B, S, AT_WIDTH), BF16),
        scratch_shapes=[ext(d, BF16) for d in DILATIONS] * 3
                       + [pltpu.VMEM((ATT_TILE, LANES), F32)] * 4
                       + [pltpu.VMEM((QBLK, QBLK + 2 * BAND), F32)],
        compiler_params=_cparams("parallel", "parallel"),
        name="dilated_attn",
    )(negm, *qs, *halo_args)


def _out_kernel(x_ref, osum_ref, qtb_ref, lst_ref, dtile_ref, g_ref, oat_ref, mq_ref, mk_ref,
                mva_ref, mvb_ref, negm_ref, onw_ref, w_ref, y_ref, sin_ref):
    nt_dims = (((1,), (1,)), ((), ()))

    @pl.when(pl.program_id(1) == 0)
    def _():
        sin_ref[...] = jnp.zeros_like(sin_ref)

    g = g_ref[...].astype(F32)
    mem = [_pair_attend(mq_ref[:, cols], mk_ref[:, cols], mva_ref[:, cols], mvb_ref[:, cols],
                        negm_ref[...])
           for cols in (slice(0, LANES), slice(LANES, 2 * LANES))]
    corr = [lax.dot_general(qtb_ref[:, hd * HG_D:(hd + 1) * HG_D], sin_ref[hd].astype(BF16),
                            nt_dims, preferred_element_type=F32) for hd in range(HG_HEADS)]
    dtile = dtile_ref[...]
    parts = []
    for hd in range(HG_HEADS):
        cols = slice(hd * HG_D, (hd + 1) * HG_D)
        sin_ref[hd] = sin_ref[hd] * dtile[:, cols] + lst_ref[hd]
        t = osum_ref[:, cols].astype(F32) + corr[hd]
        ms = jnp.mean(t * t, axis=-1, keepdims=True)
        parts.append(t * lax.rsqrt(ms + NORM_EPS) * onw_ref[...] * g[:, cols])
    parts.append(oat_ref[...].astype(F32) * g[:, HG_WIDTH:HG_WIDTH + AT_WIDTH])
    for hp, (num, den) in enumerate(mem):
        gc = slice(HG_WIDTH + AT_WIDTH + hp * LANES, HG_WIDTH + AT_WIDTH + (hp + 1) * LANES)
        parts.append(num / den * g[:, gc])
    mixed = jnp.concatenate(parts, axis=-1).astype(BF16)
    y_ref[...] = x_ref[...] + jnp.dot(mixed, w_ref[...], preferred_element_type=F32)


def _out_stage(x, o_sum, qtb, lst, dtile, g, o_at, mq, mk, mva, mvb, negm, onw, w_out):
    B, S, _ = x.shape
    ts = PROJ_TILE
    nt = S // ts
    row = lambda n: pl.BlockSpec((None, ts, n), lambda b, i: (b, nt - 1 - i, 0))
    memspec = pl.BlockSpec((None, MEM_TOKENS, MEM_WIDTH), lambda b, i: (b, 0, 0))
    return pl.pallas_call(
        _out_kernel,
        grid=(B, nt),
        in_specs=[row(D_MODEL), row(HG_WIDTH), row(HG_WIDTH),
                  pl.BlockSpec((None, None, HG_HEADS, HG_D, HG_D),
                               lambda b, i: (b, nt - 1 - i, 0, 0, 0)),
                  pl.BlockSpec((None, None, 1, HG_WIDTH), lambda b, i: (b, nt - 1 - i, 0, 0)),
                  row(1024), row(AT_WIDTH), row(MEM_WIDTH), memspec, memspec, memspec,
                  pl.BlockSpec((1, MEM_TOKENS), lambda b, i: (0, 0)),
                  pl.BlockSpec((1, HG_D), lambda b, i: (0, 0)),
                  pl.BlockSpec((D_MODEL, D_MODEL), lambda b, i: (0, 0))],
        out_specs=row(D_MODEL),
        out_shape=jax.ShapeDtypeStruct((B, S, D_MODEL), F32),
        scratch_shapes=[pltpu.VMEM((HG_HEADS, HG_D, HG_D), F32)],
        compiler_params=_cparams("parallel", "arbitrary"),
        name="out_stage",
    )(x, o_sum, qtb, lst, dtile, g, o_at, mq, mk, mva, mvb, negm, onw, w_out)


def _rope_tables(S):
    half = ROPE_DIM // 2
    pos = jnp.arange(S, dtype=F32)
    inv_freq = ROPE_THETA ** (-jnp.arange(half, dtype=F32) * 2.0 / ROPE_DIM)
    ang = pos[:, None] * inv_freq[None, :]
    cos, sin = jnp.cos(ang), jnp.sin(ang)
    pad = HEAD64 - ROPE_DIM
    cos64 = jnp.concatenate([cos, cos, jnp.ones((S, pad), F32)], axis=-1)
    sin64 = jnp.concatenate([-sin, sin, jnp.zeros((S, pad), F32)], axis=-1)
    return jnp.tile(cos64, (1, 2)), jnp.tile(sin64, (1, 2))


def _lower_bounds(p):
    sm = jax.nn.softmax(p.astype(F32), axis=0)
    return jnp.cumsum(sm, axis=0) - sm[0:1]


def _tile4(w):
    return jnp.tile(w.astype(F32), (1, 4))


def _neg_stabiliser(wq, wk, width):
    bound = (HEAD64 ** 0.5 * LOG2E) * jnp.max(jnp.abs(wq), axis=-1) * jnp.max(jnp.abs(wk), axis=-1)
    bound = jnp.minimum(bound.astype(F32), EXP2_RANGE)
    return jnp.broadcast_to(-bound[:, None, None], (DEPTH, 1, width))


def _trunk(x, mem, prm):
    B, S, _ = x.shape
    assert S % ATT_TILE == 0 and ATT_TILE % PROJ_TILE == 0 and PROJ_TILE % SCAN_CHUNK == 0
    cos_t, sin_t = _rope_tables(S)
    for l in range(DEPTH):
        p = {k: v[l] for k, v in prm.items()}
        (o_sum, qtb, lst, dtile, q1, q4, q16, k1, k4, k16, v1, v4, v16, mq, g) = _front(
            x, p["norm_w"], p["w_in"], p["lbf"], p["lbb"], p["aqw"], p["akw"], p["mqw"],
            cos_t, sin_t)
        mk, mva, mvb = _mem_kv(mem, p["mem_norm_w"], p["mem_wkv"], p["mkw"])
        o_at = _attention(p["negm_at"], (q1, q4, q16), (k1, k4, k16), (v1, v4, v16))
        x = _out_stage(x, o_sum, qtb, lst, dtile, g, o_at, mq, mk, mva, mvb, p["negm_mem"],
                       p["onw"], p["w_out"])
    return x


def kernel(x_prompt, x_sample, mem_prompt, mem_sample, norm_w, w_in, hgrn_lb_fwd, hgrn_lb_bwd,
           hgrn_onorm_w, attn_qnorm_w, attn_knorm_w, mem_norm_w, mem_wkv, mem_qnorm_w,
           mem_knorm_w, w_out):
    prm = {
        "norm_w": norm_w.astype(F32)[:, None, :],
        "w_in": w_in.astype(BF16),
        "lbf": _lower_bounds(hgrn_lb_fwd)[:, None, :],
        "lbb": _lower_bounds(hgrn_lb_bwd)[:, None, :],
        "onw": hgrn_onorm_w.astype(F32)[:, None, :],
        "aqw": _tile4(attn_qnorm_w)[:, None, :],
        "akw": _tile4(attn_knorm_w)[:, None, :],
        "mem_norm_w": mem_norm_w.astype(F32)[:, None, :],
        "mem_wkv": mem_wkv.astype(BF16),
        "mqw": _tile4(mem_qnorm_w)[:, None, :],
        "mkw": _tile4(mem_knorm_w)[:, None, :],
        "w_out": w_out.astype(BF16),
        "negm_at": _neg_stabiliser(attn_qnorm_w, attn_knorm_w, QBLK + 2 * BAND),
        "negm_mem": _neg_stabiliser(mem_qnorm_w, mem_knorm_w, MEM_TOKENS),
    }
    y_prompt = _trunk(x_prompt, mem_prompt, prm)
    y_sample = _trunk(x_sample, mem_sample, prm)
    return (y_prompt, y_sample)
```

```python
import functools

import jax
import jax.numpy as jnp
from jax import lax
from jax.experimental import pallas as pl
from jax.experimental.pallas import tpu as pltpu

F32 = jnp.float32
BF16 = jnp.bfloat16

D_MODEL = 1024
DEPTH = 4
HG_HEADS = 4
HG_D = 128
HG_WIDTH = HG_HEADS * HG_D
AT_WIDTH = 256
HEAD64 = 64
MEM_WIDTH = 256
MEM_TOKENS = 256
IN_WIDTH = 4096
DILATIONS = (1, 4, 16)
BAND = 64
ROPE_THETA = 500000.0
ROPE_DIM = 16
NORM_EPS = 1e-6
MASK_VALUE = -1e30
LANES = 128

C_HQ, C_FF, C_FB, C_HI = 0, 512, 1024, 1536
C_AQ, C_AK, C_AV, C_MQ = 2048, 2304, 2560, 2816
C_GATE = 3072
FRONT_WIDTH = C_GATE

LOG2E = 1.4426950408889634
EXP2_RANGE = 63.0
PROJ_TILE = 512
ATT_TILE = 2048
QBLK = 128
ATT_GROUP = 4
SCAN_CHUNK = 64
EXP2_CLAMP = 115.0
VMEM_LIMIT = 56 * 1024 * 1024


def _cparams(*sem):
    return pltpu.CompilerParams(dimension_semantics=sem, vmem_limit_bytes=VMEM_LIMIT)


def _sigmoid(z):
    return 1.0 / (1.0 + jnp.exp(-z))


def _lane_iota(n=LANES):
    return lax.broadcasted_iota(jnp.int32, (1, n), 1)


def _headnorm64(t, w):
    lo_mask = _lane_iota() < HEAD64
    parts = []
    for hp in range(t.shape[1] // LANES):
        th = t[:, hp * LANES:(hp + 1) * LANES]
        t2 = th * th
        lo = jnp.sum(jnp.where(lo_mask, t2, 0.0), axis=-1, keepdims=True)
        hi = jnp.sum(jnp.where(lo_mask, 0.0, t2), axis=-1, keepdims=True)
        ms = jnp.where(lo_mask, lo, hi) * (1.0 / HEAD64)
        parts.append(th * lax.rsqrt(ms + NORM_EPS))
    return jnp.concatenate(parts, axis=-1) * w


def _rope(t, cos_t, sin_t):
    lane = _lane_iota() % HEAD64
    parts = []
    for hp in range(t.shape[1] // LANES):
        th = t[:, hp * LANES:(hp + 1) * LANES]
        partner = jnp.where(lane < ROPE_DIM // 2,
                            pltpu.roll(th, LANES - ROPE_DIM // 2, 1),
                            pltpu.roll(th, ROPE_DIM // 2, 1))
        parts.append(th * cos_t + partner * sin_t)
    return jnp.concatenate(parts, axis=-1)


def _pair_attend(q2, k2, va, vb, bias):
    return _pairs_attend([(q2, k2, va, vb)], bias)[0]


def _pairs_attend(pairs, bias):
    lo_mask = _lane_iota() < HEAD64
    nt = (((1,), (1,)), ((), ()))
    s = []
    for q2, k2, _, _ in pairs:
        zero = jnp.zeros_like(q2)
        s.append((lax.dot_general(jnp.where(lo_mask, q2, zero), k2, nt, preferred_element_type=F32),
                  lax.dot_general(jnp.where(lo_mask, zero, q2), k2, nt, preferred_element_type=F32)))
    p = [(jnp.exp2(s_lo + bias).astype(BF16), jnp.exp2(s_hi + bias).astype(BF16)) for s_lo, s_hi in s]
    o = [(jnp.dot(p_lo, va, preferred_element_type=F32), jnp.dot(p_hi, vb, preferred_element_type=F32))
         for (p_lo, p_hi), (_, _, va, vb) in zip(p, pairs)]
    return [(jnp.where(lo_mask, o_lo, o_hi), pltpu.roll(jnp.where(lo_mask, o_hi, o_lo), HEAD64, 1))
            for o_lo, o_hi in o]


def _front_kernel(x_ref, nw_ref, w_ref, lbf_ref, lbb_ref, aqw_ref, akw_ref, mqw_ref,
                  cos_ref, sin_ref,
                  osum_o, qtb_o, lst_o, dtile_o,
                  q1_o, q4_o, q16_o, k1_o, k4_o, k16_o, v1_o, v4_o, v16_o, mq_o,
                  st_ref, tmp_lo, tmp_hi):
    ts = x_ref.shape[0]
    C = SCAN_CHUNK
    chunks = range(ts // C)
    heads = range(HG_HEADS)
    nt_dims = (((1,), (1,)), ((), ()))
    tn_dims = (((0,), (0,)), ((), ()))
    row_sl = [slice(c * C, (c + 1) * C) for c in chunks]
    col_sl = [slice(hd * HG_D, (hd + 1) * HG_D) for hd in heads]

    @pl.when(pl.program_id(1) == 0)
    def _():
        st_ref[...] = jnp.zeros_like(st_ref)

    x = x_ref[...]
    ms = jnp.mean(x * x, axis=-1, keepdims=True)
    h = (x * lax.rsqrt(ms + NORM_EPS) * nw_ref[...]).astype(BF16)

    def proj(c0, n):
        return jnp.dot(h, w_ref[:, c0:c0 + n], preferred_element_type=F32)

    def gate_terms(z, lb_ref):
        lb = lb_ref[...]
        sig = _sigmoid(z)
        return jnp.log2(lb + (1.0 - lb) * sig), (1.0 - lb) * (1.0 - sig)

    z_q = proj(C_HQ, HG_WIDTH)
    z_i = proj(C_HI, HG_WIDTH)
    q = z_q * _sigmoid(z_q)
    z_f = proj(C_FF, HG_WIDTH)
    v_bf = z_i.astype(BF16)
    z_b = proj(C_FB, HG_WIDTH)
    lf_f, kk_f = gate_terms(z_f, lbf_ref)
    z_aq = proj(C_AQ, AT_WIDTH)
    z_ak = proj(C_AK, AT_WIDTH)
    lf_b, kk_b = gate_terms(z_b, lbb_ref)

    ri = lax.broadcasted_iota(jnp.int32, (C, C), 0)
    ci = lax.broadcasted_iota(jnp.int32, (C, C), 1)
    tri = [ci <= ri, ci >= ri]
    tri_bf = [jnp.where(t, 1.0, 0.0).astype(BF16) for t in tri]
    end_row = [C - 1, 0]
    mid_row = [C // 2 - 1, C // 2]
    lf_d, kk_d = [lf_f, lf_b], [kk_f, kk_b]
    dirs = range(2)

    b_all = [[None] * len(chunks) for _ in dirs]
    for d in dirs:
        for c in chunks:
            lf = lf_d[d][row_sl[c], :]
            lf_hi = lf.astype(BF16)
            lf_lo = (lf - lf_hi.astype(F32)).astype(BF16)
            b_all[d][c] = (jnp.dot(tri_bf[d], lf_hi, preferred_element_type=F32)
                           + jnp.dot(tri_bf[d], lf_lo, preferred_element_type=F32))
    z_av = proj(C_AV, AT_WIDTH)
    z_mq = proj(C_MQ, MEM_WIDTH)

    q_inter = [[None] * len(chunks) for _ in dirs]
    k_state = [[None] * len(chunks) for _ in dirs]
    q_intra = [[None] * len(chunks) for _ in dirs]
    k_intra = [[None] * len(chunks) for _ in dirs]
    dec = [[None] * len(chunks) for _ in dirs]
    for d in dirs:
        for c in chunks:
            b = b_all[d][c]
            qc = q[row_sl[c], :]
            kc = kk_d[d][row_sl[c], :]
            b_end = b[end_row[d]:end_row[d] + 1, :]
            b_mid = b[mid_row[d]:mid_row[d] + 1, :]
            dec[d][c] = jnp.exp2(b_end)
            q_inter[d][c] = (qc * jnp.exp2(b)).astype(BF16)
            k_state[d][c] = (kc * jnp.exp2(b_end - b)).astype(BF16)
            q_intra[d][c] = (qc * jnp.exp2(jnp.minimum(b - b_mid, EXP2_CLAMP))).astype(BF16)
            k_intra[d][c] = (kc * jnp.exp2(jnp.minimum(b_mid - b, EXP2_CLAMP))).astype(BF16)
    off = jnp.zeros((1, HG_WIDTH), F32)
    for c in reversed(chunks):
        b = b_all[1][c]
        qtb_o[row_sl[c], :] = (q[row_sl[c], :] * jnp.exp2(b + off)).astype(BF16)
        off = off + b[end_row[1]:end_row[1] + 1, :]
    dtile_o[...] = jnp.exp2(off)

    sc = [[[lax.dot_general(q_intra[d][c][:, col_sl[hd]], k_intra[d][c][:, col_sl[hd]], nt_dims,
                            preferred_element_type=F32) for hd in heads] for c in chunks]
          for d in dirs]
    upd2 = [[lax.dot_general(v_bf[row_sl[c], col_sl[hd]],
                             jnp.concatenate([k_state[d][c][:, col_sl[hd]] for d in dirs], axis=1),
                             tn_dims, preferred_element_type=F32) for hd in heads] for c in chunks]
    upd = [[[upd2[c][hd][:, d * HG_D:(d + 1) * HG_D] for hd in heads] for c in chunks]
           for d in dirs]
    sc = [[[jnp.where(tri[d], sc[d][c][hd], 0.0).astype(BF16) for hd in heads] for c in chunks]
          for d in dirs]

    st_in = [[[None] * HG_HEADS for _ in chunks] for _ in dirs]
    for hd in heads:
        st = st_ref[hd]
        for c in chunks:
            st_in[0][c][hd] = st.astype(BF16)
            st = st * dec[0][c][:, col_sl[hd]] + upd[0][c][hd]
        st_ref[hd] = st
        st = None
        for c in reversed(chunks):
            if st is None:
                st = upd[1][c][hd]
            else:
                st_in[1][c][hd] = st.astype(BF16)
                st = st * dec[1][c][:, col_sl[hd]] + upd[1][c][hd]
        lst_o[hd] = st
    o_parts = [[[] for _ in heads] for _ in chunks]
    for d in dirs:
        for c in chunks:
            for hd in heads:
                o_parts[c][hd].append(jnp.dot(sc[d][c][hd], v_bf[row_sl[c], col_sl[hd]],
                                              preferred_element_type=F32))
                if st_in[d][c][hd] is not None:
                    o_parts[c][hd].append(lax.dot_general(q_inter[d][c][:, col_sl[hd]],
                                                          st_in[d][c][hd], nt_dims,
                                                          preferred_element_type=F32))

    def emit(val, o1, o4, o16):
        o1[...] = val.astype(BF16)
        for hp, tmp in enumerate((tmp_lo, tmp_hi)):
            cols = slice(hp * LANES, (hp + 1) * LANES)
            tmp[...] = val[:, cols]
            for dil, o in ((4, o4), (16, o16)):
                for r in range(dil):
                    o[r, :, cols] = tmp[pl.ds(r, ts // dil, stride=dil), :].astype(BF16)

    cos_t = cos_ref[...]
    sin_t = sin_ref[...]
    scale = LOG2E * HEAD64 ** -0.5
    emit(_rope(_headnorm64(z_aq, aqw_ref[...]), cos_t, sin_t) * scale, q1_o, q4_o, q16_o)
    emit(_rope(_headnorm64(z_ak, akw_ref[...]), cos_t, sin_t), k1_o, k4_o, k16_o)
    emit(z_av, v1_o, v4_o, v16_o)
    mq_o[...] = (_headnorm64(z_mq, mqw_ref[...]) * scale).astype(BF16)
    for c in chunks:
        for hd in heads:
            o_sum = o_parts[c][hd][0]
            for part in o_parts[c][hd][1:]:
                o_sum = o_sum + part
            osum_o[row_sl[c], col_sl[hd]] = o_sum.astype(BF16)


def _front(x, nw, w_in, lbf, lbb, aqw, akw, mqw, cos_t, sin_t):
    B, S, _ = x.shape
    ts = PROJ_TILE
    nt = S // ts
    per_att = ATT_TILE // ts
    row = lambda n: pl.BlockSpec((None, ts, n), lambda b, i: (b, i, 0))
    vec = lambda n: pl.BlockSpec((1, n), lambda b, i: (0, 0))
    tab = pl.BlockSpec((ts, LANES), lambda b, i: (i, 0))

    def grouped(dil):
        spec = pl.BlockSpec((None, None, dil, ts // dil, AT_WIDTH),
                            lambda b, i: (b, i // per_att, 0, i % per_att, 0))
        shape = jax.ShapeDtypeStruct((B, S // ATT_TILE, dil, ATT_TILE // dil, AT_WIDTH), BF16)
        return spec, shape

    outs = [(row(HG_WIDTH), jax.ShapeDtypeStruct((B, S, HG_WIDTH), BF16)),
            (row(HG_WIDTH), jax.ShapeDtypeStruct((B, S, HG_WIDTH), BF16)),
            (pl.BlockSpec((None, None, HG_HEADS, HG_D, HG_D), lambda b, i: (b, i, 0, 0, 0)),
             jax.ShapeDtypeStruct((B, nt, HG_HEADS, HG_D, HG_D), F32)),
            (pl.BlockSpec((None, None, 1, HG_WIDTH), lambda b, i: (b, i, 0, 0)),
             jax.ShapeDtypeStruct((B, nt, 1, HG_WIDTH), F32))]
    for _ in range(3):
        outs.append((row(AT_WIDTH), jax.ShapeDtypeStruct((B, S, AT_WIDTH), BF16)))
        outs.append(grouped(4))
        outs.append(grouped(16))
    outs.append((row(MEM_WIDTH), jax.ShapeDtypeStruct((B, S, MEM_WIDTH), BF16)))
    return pl.pallas_call(
        _front_kernel,
        grid=(B, nt),
        in_specs=[row(D_MODEL), vec(D_MODEL),
                  pl.BlockSpec((D_MODEL, FRONT_WIDTH), lambda b, i: (0, 0)),
                  vec(HG_WIDTH), vec(HG_WIDTH), vec(AT_WIDTH), vec(AT_WIDTH), vec(MEM_WIDTH),
                  tab, tab],
        out_specs=[s for s, _ in outs],
        out_shape=[s for _, s in outs],
        scratch_shapes=[pltpu.VMEM((HG_HEADS, HG_D, HG_D), F32)]
                       + [pltpu.VMEM((ts, LANES), F32)] * 2,
        compiler_params=_cparams("parallel", "arbitrary"),
        name="front",
    )(x, nw, w_in, lbf, lbb, aqw, akw, mqw, cos_t, sin_t)


def _mem_kv_kernel(m_ref, nw_ref, w_ref, kw_ref, mk_o, mva_o, mvb_o):
    x = m_ref[...]
    ms = jnp.mean(x * x, axis=-1, keepdims=True)
    h = (x * lax.rsqrt(ms + NORM_EPS) * nw_ref[...]).astype(BF16)
    kv = jnp.dot(h, w_ref[...], preferred_element_type=F32)
    mk_o[...] = _headnorm64(kv[:, :MEM_WIDTH], kw_ref[...]).astype(BF16)
    lo_mask = (lax.broadcasted_iota(jnp.int32, (1, MEM_WIDTH), 1) % LANES) < HEAD64
    mv = kv[:, MEM_WIDTH:]
    mva_o[...] = jnp.where(lo_mask, mv, 1.0).astype(BF16)
    mvb_o[...] = jnp.where(lo_mask, 1.0, mv).astype(BF16)


def _mem_kv(mem, nw, wkv, kw):
    B, M, _ = mem.shape
    out = pl.BlockSpec((None, M, MEM_WIDTH), lambda b: (b, 0, 0))
    return pl.pallas_call(
        _mem_kv_kernel,
        grid=(B,),
        in_specs=[pl.BlockSpec((None, M, D_MODEL), lambda b: (b, 0, 0)),
                  pl.BlockSpec((1, D_MODEL), lambda b: (0, 0)),
                  pl.BlockSpec((D_MODEL, 2 * MEM_WIDTH), lambda b: (0, 0)),
                  pl.BlockSpec((1, MEM_WIDTH), lambda b: (0, 0))],
        out_specs=[out, out, out],
        out_shape=[jax.ShapeDtypeStruct((B, M, MEM_WIDTH), BF16)] * 3,
        compiler_params=_cparams("parallel"),
        name="mem_kv",
    )(mem, nw, wkv, kw)


def _attn_kernel(negm_ref, q1_ref, q4_ref, q16_ref,
                 k1p, k1m, k1n, k4p, k4m, k4n, k16p, k16m, k16n,
                 v1p, v1m, v1n, v4p, v4m, v4n, v16p, v16m, v16n,
                 o_ref,
                 kx1, kx4, kx16, va1, va4, va16, vb1, vb4, vb16,
                 num_lo, num_hi, den_lo, den_hi, bias_s, *, nt):
    num_s = (num_lo, num_hi)
    den_s = (den_lo, den_hi)
    i = pl.program_id(1)
    first = i == 0
    last = i == nt - 1
    lo_mask = _lane_iota(AT_WIDTH) % LANES < HEAD64

    qi = lax.broadcasted_iota(jnp.int32, (QBLK, QBLK + 2 * BAND), 0)
    ke = lax.broadcasted_iota(jnp.int32, (QBLK, QBLK + 2 * BAND), 1)
    delta = ke - BAND - qi
    bias_s[...] = jnp.where((delta <= BAND) & (delta >= -BAND), negm_ref[...], MASK_VALUE)

    def fill(kx, va, vb, kp, km, kn, vp, vm, vn):
        nres, rows = km.shape[0], km.shape[1]
        for r in range(nres):
            for dst, src, edge in ((slice(0, BAND), (kp, vp), first),
                                   (slice(BAND, BAND + rows), (km, vm), None),
                                   (slice(BAND + rows, rows + 2 * BAND), (kn, vn), last)):
                k = src[0][r]
                v = src[1][r]
                v_a = jnp.where(lo_mask, v, jnp.ones_like(v))
                v_b = jnp.where(lo_mask, jnp.ones_like(v), v)
                if edge is not None:
                    zero = jnp.zeros_like(k)
                    k = jnp.where(edge, zero, k)
                    v_a = jnp.where(edge, zero, v_a)
                    v_b = jnp.where(edge, zero, v_b)
                kx[r, dst, :] = k
                va[r, dst, :] = v_a
                vb[r, dst, :] = v_b

    fill(kx1, va1, vb1, k1p, k1m, k1n, v1p, v1m, v1n)
    fill(kx4, va4, vb4, k4p, k4m, k4n, v4p, v4m, v4n)
    fill(kx16, va16, vb16, k16p, k16m, k16n, v16p, v16m, v16n)

    def pattern(dil, q_ref, kx, va, vb):
        blocks_per_res = ATT_TILE // dil // QBLK

        def body(grp, carry):
            pairs, dests = [], []
            for u in range(ATT_GROUP):
                it = grp * ATT_GROUP + u
                res = it // blocks_per_res
                a = pl.multiple_of((it % blocks_per_res) * QBLK, QBLK)
                out_rows = pl.ds(a, QBLK) if dil == 1 else pl.ds(a * dil + res, QBLK, stride=dil)
                win = pl.ds(a, QBLK + 2 * BAND)
                for hp in range(AT_WIDTH // LANES):
                    cols = slice(hp * LANES, (hp + 1) * LANES)
                    pairs.append((q_ref[res, pl.ds(a, QBLK), cols], kx[res, win, cols],
                                  va[res, win, cols], vb[res, win, cols]))
                    dests.append((hp, out_rows))
            for (hp, out_rows), (num, den) in zip(dests, _pairs_attend(pairs, bias_s[...])):
                if dil == 1:
                    num_s[hp][out_rows, :] = num
                    den_s[hp][out_rows, :] = den
                else:
                    num_s[hp][out_rows, :] += num
                    den_s[hp][out_rows, :] += den
            return carry

        lax.fori_loop(0, dil * blocks_per_res // ATT_GROUP, body, 0)

    pattern(1, q1_ref, kx1, va1, vb1)
    pattern(4, q4_ref, kx4, va4, vb4)
    pattern(16, q16_ref, kx16, va16, vb16)
    for hp in range(AT_WIDTH // LANES):
        o_ref[:, hp * LANES:(hp + 1) * LANES] = (num_s[hp][...] / den_s[hp][...]).astype(BF16)


def _attention(negm, q, k, v):
    B, S, _ = q[0].shape
    nt = S // ATT_TILE

    def view(t, dil):
        return t.reshape(B, nt, dil, ATT_TILE // dil, AT_WIDTH)

    def main(dil):
        return pl.BlockSpec((None, None, dil, ATT_TILE // dil, AT_WIDTH),
                            lambda b, i: (b, i, 0, 0, 0))

    def prev(dil):
        lastblk = ATT_TILE // dil // BAND - 1
        return pl.BlockSpec((None, None, dil, BAND, AT_WIDTH),
                            lambda b, i: (b, jnp.maximum(i - 1, 0), 0, lastblk, 0))

    def nxt(dil):
        return pl.BlockSpec((None, None, dil, BAND, AT_WIDTH),
                            lambda b, i: (b, jnp.minimum(i + 1, nt - 1), 0, 0, 0))

    qs = [view(t, d) for t, d in zip(q, DILATIONS)]
    ks = [view(t, d) for t, d in zip(k, DILATIONS)]
    vs = [view(t, d) for t, d in zip(v, DILATIONS)]
    halo_specs, halo_args = [], []
    for ts_ in (ks, vs):
        for t, d in zip(ts_, DILATIONS):
            halo_specs += [prev(d), main(d), nxt(d)]
            halo_args += [t, t, t]
    ext = lambda d, dt: pltpu.VMEM((d, ATT_TILE // d + 2 * BAND, AT_WIDTH), dt)
    return pl.pallas_call(
        functools.partial(_attn_kernel, nt=nt),
        grid=(B, nt),
        in_specs=[pl.BlockSpec((1, QBLK + 2 * BAND), lambda b, i: (0, 0))]
                 + [main(d) for d in DILATIONS] + halo_specs,
        out_specs=pl.BlockSpec((None, ATT_TILE, AT_WIDTH), lambda b, i: (b, i, 0)),
        out_shape=jax.ShapeDtypeStruct((B, S, AT_WIDTH), BF16),
        scratch_shapes=[ext(d, BF16) for d in DILATIONS] * 3
                       + [pltpu.VMEM((ATT_TILE, LANES), F32)] * 4
                       + [pltpu.VMEM((QBLK, QBLK + 2 * BAND), F32)],
        compiler_params=_cparams("parallel", "parallel"),
        name="dilated_attn",
    )(negm, *qs, *halo_args)


def _out_kernel(x_ref, nw_ref, wg_ref, osum_ref, qtb_ref, lst_ref, dtile_ref, oat_ref, mq_ref,
                mk_ref, mva_ref, mvb_ref, negm_ref, onw_ref, w_ref, y_ref, sin_ref):
    nt_dims = (((1,), (1,)), ((), ()))

    @pl.when(pl.program_id(1) == 0)
    def _():
        sin_ref[...] = jnp.zeros_like(sin_ref)

    x = x_ref[...]
    ms = jnp.mean(x * x, axis=-1, keepdims=True)
    h = (x * lax.rsqrt(ms + NORM_EPS) * nw_ref[...]).astype(BF16)
    z_g = jnp.dot(h, wg_ref[...], preferred_element_type=F32)
    mem = [_pair_attend(mq_ref[:, cols], mk_ref[:, cols], mva_ref[:, cols], mvb_ref[:, cols],
                        negm_ref[...])
           for cols in (slice(0, LANES), slice(LANES, 2 * LANES))]
    corr = [lax.dot_general(qtb_ref[:, hd * HG_D:(hd + 1) * HG_D], sin_ref[hd].astype(BF16),
                            nt_dims, preferred_element_type=F32) for hd in range(HG_HEADS)]
    dtile = dtile_ref[...]
    g = z_g * _sigmoid(z_g)
    parts = []
    for hd in range(HG_HEADS):
        cols = slice(hd * HG_D, (hd + 1) * HG_D)
        sin_ref[hd] = sin_ref[hd] * dtile[:, cols] + lst_ref[hd]
        t = osum_ref[:, cols].astype(F32) + corr[hd]
        ms = jnp.mean(t * t, axis=-1, keepdims=True)
        parts.append(t * lax.rsqrt(ms + NORM_EPS) * onw_ref[...] * g[:, cols])
    parts.append(oat_ref[...].astype(F32) * g[:, HG_WIDTH:HG_WIDTH + AT_WIDTH])
    for hp, (num, den) in enumerate(mem):
        gc = slice(HG_WIDTH + AT_WIDTH + hp * LANES, HG_WIDTH + AT_WIDTH + (hp + 1) * LANES)
        parts.append(num / den * g[:, gc])
    mixed = jnp.concatenate(parts, axis=-1).astype(BF16)
    y_ref[...] = x + jnp.dot(mixed, w_ref[...], preferred_element_type=F32)


def _out_stage(x, nw, w_in, o_sum, qtb, lst, dtile, o_at, mq, mk, mva, mvb, negm, onw, w_out):
    B, S, _ = x.shape
    ts = PROJ_TILE
    nt = S // ts
    row = lambda n: pl.BlockSpec((None, ts, n), lambda b, i: (b, nt - 1 - i, 0))
    memspec = pl.BlockSpec((None, MEM_TOKENS, MEM_WIDTH), lambda b, i: (b, 0, 0))
    return pl.pallas_call(
        _out_kernel,
        grid=(B, nt),
        in_specs=[row(D_MODEL), pl.BlockSpec((1, D_MODEL), lambda b, i: (0, 0)),
                  pl.BlockSpec((D_MODEL, IN_WIDTH - C_GATE),
                               lambda b, i: (0, C_GATE // (IN_WIDTH - C_GATE))),
                  row(HG_WIDTH), row(HG_WIDTH),
                  pl.BlockSpec((None, None, HG_HEADS, HG_D, HG_D),
                               lambda b, i: (b, nt - 1 - i, 0, 0, 0)),
                  pl.BlockSpec((None, None, 1, HG_WIDTH), lambda b, i: (b, nt - 1 - i, 0, 0)),
                  row(AT_WIDTH), row(MEM_WIDTH), memspec, memspec, memspec,
                  pl.BlockSpec((1, MEM_TOKENS), lambda b, i: (0, 0)),
                  pl.BlockSpec((1, HG_D), lambda b, i: (0, 0)),
                  pl.BlockSpec((D_MODEL, D_MODEL), lambda b, i: (0, 0))],
        out_specs=row(D_MODEL),
        out_shape=jax.ShapeDtypeStruct((B, S, D_MODEL), F32),
        scratch_shapes=[pltpu.VMEM((HG_HEADS, HG_D, HG_D), F32)],
        compiler_params=_cparams("parallel", "arbitrary"),
        name="out_stage",
    )(x, nw, w_in, o_sum, qtb, lst, dtile, o_at, mq, mk, mva, mvb, negm, onw, w_out)


def _rope_tables(S):
    half = ROPE_DIM // 2
    pos = jnp.arange(S, dtype=F32)
    inv_freq = ROPE_THETA ** (-jnp.arange(half, dtype=F32) * 2.0 / ROPE_DIM)
    ang = pos[:, None] * inv_freq[None, :]
    cos, sin = jnp.cos(ang), jnp.sin(ang)
    pad = HEAD64 - ROPE_DIM
    cos64 = jnp.concatenate([cos, cos, jnp.ones((S, pad), F32)], axis=-1)
    sin64 = jnp.concatenate([-sin, sin, jnp.zeros((S, pad), F32)], axis=-1)
    return jnp.tile(cos64, (1, 2)), jnp.tile(sin64, (1, 2))


def _lower_bounds(p):
    sm = jax.nn.softmax(p.astype(F32), axis=0)
    return jnp.cumsum(sm, axis=0) - sm[0:1]


def _tile4(w):
    return jnp.tile(w.astype(F32), (1, 4))


def _neg_stabiliser(wq, wk, width):
    bound = (HEAD64 ** 0.5 * LOG2E) * jnp.max(jnp.abs(wq), axis=-1) * jnp.max(jnp.abs(wk), axis=-1)
    bound = jnp.minimum(bound.astype(F32), EXP2_RANGE)
    return jnp.broadcast_to(-bound[:, None, None], (DEPTH, 1, width))


def _trunk(x, mem, prm):
    B, S, _ = x.shape
    assert S % ATT_TILE == 0 and ATT_TILE % PROJ_TILE == 0 and PROJ_TILE % SCAN_CHUNK == 0
    cos_t, sin_t = _rope_tables(S)
    for l in range(DEPTH):
        p = {k: v[l] for k, v in prm.items()}
        (o_sum, qtb, lst, dtile, q1, q4, q16, k1, k4, k16, v1, v4, v16, mq) = _front(
            x, p["norm_w"], p["w_in"], p["lbf"], p["lbb"], p["aqw"], p["akw"], p["mqw"],
            cos_t, sin_t)
        mk, mva, mvb = _mem_kv(mem, p["mem_norm_w"], p["mem_wkv"], p["mkw"])
        o_at = _attention(p["negm_at"], (q1, q4, q16), (k1, k4, k16), (v1, v4, v16))
        x = _out_stage(x, p["norm_w"], p["w_in"], o_sum, qtb, lst, dtile, o_at, mq, mk, mva, mvb,
                       p["negm_mem"], p["onw"], p["w_out"])
    return x


def kernel(x_prompt, x_sample, mem_prompt, mem_sample, norm_w, w_in, hgrn_lb_fwd, hgrn_lb_bwd,
           hgrn_onorm_w, attn_qnorm_w, attn_knorm_w, mem_norm_w, mem_wkv, mem_qnorm_w,
           mem_knorm_w, w_out):
    prm = {
        "norm_w": norm_w.astype(F32)[:, None, :],
        "w_in": w_in.astype(BF16),
        "lbf": _lower_bounds(hgrn_lb_fwd)[:, None, :],
        "lbb": _lower_bounds(hgrn_lb_bwd)[:, None, :],
        "onw": hgrn_onorm_w.astype(F32)[:, None, :],
        "aqw": _tile4(attn_qnorm_w)[:, None, :],
        "akw": _tile4(attn_knorm_w)[:, None, :],
        "mem_norm_w": mem_norm_w.astype(F32)[:, None, :],
        "mem_wkv": mem_wkv.astype(BF16),
        "mqw": _tile4(mem_qnorm_w)[:, None, :],
        "mkw": _tile4(mem_knorm_w)[:, None, :],
        "w_out": w_out.astype(BF16),
        "negm_at": _neg_stabiliser(attn_qnorm_w, attn_knorm_w, QBLK + 2 * BAND),
        "negm_mem": _neg_stabiliser(mem_qnorm_w, mem_knorm_w, MEM_TOKENS),
    }
    y_prompt = _trunk(x_prompt, mem_prompt, prm)
    y_sample = _trunk(x_sample, mem_sample, prm)
    return (y_prompt, y_sample)
```

```python
import functools

import jax
import jax.numpy as jnp
from jax import lax
from jax.experimental import pallas as pl
from jax.experimental.pallas import tpu as pltpu

F32 = jnp.float32
BF16 = jnp.bfloat16

D_MODEL = 1024
DEPTH = 4
HG_HEADS = 4
HG_D = 128
HG_WIDTH = HG_HEADS * HG_D
AT_WIDTH = 256
HEAD64 = 64
MEM_WIDTH = 256
MEM_TOKENS = 256
IN_WIDTH = 4096
DILATIONS = (1, 4, 16)
BAND = 64
ROPE_THETA = 500000.0
ROPE_DIM = 16
NORM_EPS = 1e-6
MASK_VALUE = -1e30
LANES = 128

C_HQ, C_FF, C_FB, C_HI = 0, 512, 1024, 1536
C_AQ, C_AK, C_AV, C_MQ = 2048, 2304, 2560, 2816
C_GATE = 3072
FRONT_WIDTH = C_GATE

LOG2E = 1.4426950408889634
EXP2_RANGE = 63.0
PROJ_TILE = 512
ATT_TILE = 2048
QBLK = 128
ATT_GROUP = 8
SCAN_CHUNK = 64
EXP2_CLAMP = 115.0
VMEM_LIMIT = 56 * 1024 * 1024


def _cparams(*sem):
    return pltpu.CompilerParams(dimension_semantics=sem, vmem_limit_bytes=VMEM_LIMIT)


def _sigmoid(z):
    return 1.0 / (1.0 + jnp.exp(-z))


def _lane_iota(n=LANES):
    return lax.broadcasted_iota(jnp.int32, (1, n), 1)


def _headnorm64(t, w):
    lo_mask = _lane_iota() < HEAD64
    parts = []
    for hp in range(t.shape[1] // LANES):
        th = t[:, hp * LANES:(hp + 1) * LANES]
        t2 = th * th
        lo = jnp.sum(jnp.where(lo_mask, t2, 0.0), axis=-1, keepdims=True)
        hi = jnp.sum(jnp.where(lo_mask, 0.0, t2), axis=-1, keepdims=True)
        ms = jnp.where(lo_mask, lo, hi) * (1.0 / HEAD64)
        parts.append(th * lax.rsqrt(ms + NORM_EPS))
    return jnp.concatenate(parts, axis=-1) * w


def _rope(t, cos_t, sin_t):
    lane = _lane_iota() % HEAD64
    parts = []
    for hp in range(t.shape[1] // LANES):
        th = t[:, hp * LANES:(hp + 1) * LANES]
        partner = jnp.where(lane < ROPE_DIM // 2,
                            pltpu.roll(th, LANES - ROPE_DIM // 2, 1),
                            pltpu.roll(th, ROPE_DIM // 2, 1))
        parts.append(th * cos_t + partner * sin_t)
    return jnp.concatenate(parts, axis=-1)


def _pair_attend(q2, k2, va, vb, bias):
    return _pairs_attend([(q2, k2, va, vb, bias)])[0]


def _pairs_attend(pairs):
    lo_mask = _lane_iota() < HEAD64
    nt = (((1,), (1,)), ((), ()))
    s = []
    for q2, k2, _, _, _ in pairs:
        zero = jnp.zeros_like(q2)
        s.append((lax.dot_general(jnp.where(lo_mask, q2, zero), k2, nt, preferred_element_type=F32),
                  lax.dot_general(jnp.where(lo_mask, zero, q2), k2, nt, preferred_element_type=F32)))
    p = [(jnp.exp2(s_lo + pr[4]).astype(BF16), jnp.exp2(s_hi + pr[4]).astype(BF16))
         for (s_lo, s_hi), pr in zip(s, pairs)]
    o = [(jnp.dot(p_lo, pr[2], preferred_element_type=F32),
          jnp.dot(p_hi, pr[3], preferred_element_type=F32)) for (p_lo, p_hi), pr in zip(p, pairs)]
    return [(jnp.where(lo_mask, o_lo, o_hi), pltpu.roll(jnp.where(lo_mask, o_hi, o_lo), HEAD64, 1))
            for o_lo, o_hi in o]


def _front_kernel(x_ref, nw_ref, w_ref, lbf_ref, lbb_ref, aqw_ref, akw_ref, mqw_ref,
                  cos_ref, sin_ref,
                  osum_o, qtb_o, lst_o, dtile_o,
                  q1_o, q4_o, q16_o, k1_o, k4_o, k16_o, v1_o, v4_o, v16_o, mq_o,
                  st_ref, tmp_lo, tmp_hi):
    ts = x_ref.shape[0]
    C = SCAN_CHUNK
    chunks = range(ts // C)
    heads = range(HG_HEADS)
    nt_dims = (((1,), (1,)), ((), ()))
    tn_dims = (((0,), (0,)), ((), ()))
    row_sl = [slice(c * C, (c + 1) * C) for c in chunks]
    col_sl = [slice(hd * HG_D, (hd + 1) * HG_D) for hd in heads]

    @pl.when(pl.program_id(1) == 0)
    def _():
        st_ref[...] = jnp.zeros_like(st_ref)

    x = x_ref[...]
    ms = jnp.mean(x * x, axis=-1, keepdims=True)
    h = (x * lax.rsqrt(ms + NORM_EPS) * nw_ref[...]).astype(BF16)

    def proj(c0, n):
        return jnp.dot(h, w_ref[:, c0:c0 + n], preferred_element_type=F32)

    def gate_terms(z, lb_ref):
        lb = lb_ref[...]
        sig = _sigmoid(z)
        return jnp.log2(lb + (1.0 - lb) * sig), (1.0 - lb) * (1.0 - sig)

    z_q = proj(C_HQ, HG_WIDTH)
    z_i = proj(C_HI, HG_WIDTH)
    q = z_q * _sigmoid(z_q)
    z_f = proj(C_FF, HG_WIDTH)
    v_bf = z_i.astype(BF16)
    z_b = proj(C_FB, HG_WIDTH)
    lf_f, kk_f = gate_terms(z_f, lbf_ref)
    z_aq = proj(C_AQ, AT_WIDTH)
    z_ak = proj(C_AK, AT_WIDTH)
    lf_b, kk_b = gate_terms(z_b, lbb_ref)

    ri = lax.broadcasted_iota(jnp.int32, (C, C), 0)
    ci = lax.broadcasted_iota(jnp.int32, (C, C), 1)
    tri = [ci <= ri, ci >= ri]
    tri_bf = [jnp.where(t, 1.0, 0.0).astype(BF16) for t in tri]
    end_row = [C - 1, 0]
    mid_row = [C // 2 - 1, C // 2]
    lf_d, kk_d = [lf_f, lf_b], [kk_f, kk_b]
    dirs = range(2)

    b_all = [[None] * len(chunks) for _ in dirs]
    for d in dirs:
        for c in chunks:
            lf = lf_d[d][row_sl[c], :]
            lf_hi = lf.astype(BF16)
            lf_lo = (lf - lf_hi.astype(F32)).astype(BF16)
            b_all[d][c] = (jnp.dot(tri_bf[d], lf_hi, preferred_element_type=F32)
                           + jnp.dot(tri_bf[d], lf_lo, preferred_element_type=F32))
    z_av = proj(C_AV, AT_WIDTH)
    z_mq = proj(C_MQ, MEM_WIDTH)

    q_inter = [[None] * len(chunks) for _ in dirs]
    k_state = [[None] * len(chunks) for _ in dirs]
    q_intra = [[None] * len(chunks) for _ in dirs]
    k_intra = [[None] * len(chunks) for _ in dirs]
    dec = [[None] * len(chunks) for _ in dirs]
    for d in dirs:
        for c in chunks:
            b = b_all[d][c]
            qc = q[row_sl[c], :]
            kc = kk_d[d][row_sl[c], :]
            b_end = b[end_row[d]:end_row[d] + 1, :]
            b_mid = b[mid_row[d]:mid_row[d] + 1, :]
            dec[d][c] = jnp.exp2(b_end)
            q_inter[d][c] = (qc * jnp.exp2(b)).astype(BF16)
            k_state[d][c] = (kc * jnp.exp2(b_end - b)).astype(BF16)
            q_intra[d][c] = (qc * jnp.exp2(jnp.minimum(b - b_mid, EXP2_CLAMP))).astype(BF16)
            k_intra[d][c] = (kc * jnp.exp2(jnp.minimum(b_mid - b, EXP2_CLAMP))).astype(BF16)
    off = jnp.zeros((1, HG_WIDTH), F32)
    for c in reversed(chunks):
        b = b_all[1][c]
        qtb_o[row_sl[c], :] = (q[row_sl[c], :] * jnp.exp2(b + off)).astype(BF16)
        off = off + b[end_row[1]:end_row[1] + 1, :]
    dtile_o[...] = jnp.exp2(off)

    sc = [[[lax.dot_general(q_intra[d][c][:, col_sl[hd]], k_intra[d][c][:, col_sl[hd]], nt_dims,
                            preferred_element_type=F32) for hd in heads] for c in chunks]
          for d in dirs]
    upd2 = [[lax.dot_general(v_bf[row_sl[c], col_sl[hd]],
                             jnp.concatenate([k_state[d][c][:, col_sl[hd]] for d in dirs], axis=1),
                             tn_dims, preferred_element_type=F32) for hd in heads] for c in chunks]
    upd = [[[upd2[c][hd][:, d * HG_D:(d + 1) * HG_D] for hd in heads] for c in chunks]
           for d in dirs]
    sc = [[[jnp.where(tri[d], sc[d][c][hd], 0.0).astype(BF16) for hd in heads] for c in chunks]
          for d in dirs]

    st_in = [[[None] * HG_HEADS for _ in chunks] for _ in dirs]
    for hd in heads:
        st = st_ref[hd]
        for c in chunks:
            st_in[0][c][hd] = st.astype(BF16)
            st = st * dec[0][c][:, col_sl[hd]] + upd[0][c][hd]
        st_ref[hd] = st
        st = None
        for c in reversed(chunks):
            if st is None:
                st = upd[1][c][hd]
            else:
                st_in[1][c][hd] = st.astype(BF16)
                st = st * dec[1][c][:, col_sl[hd]] + upd[1][c][hd]
        lst_o[hd] = st
    o_parts = [[[] for _ in heads] for _ in chunks]
    for d in dirs:
        for c in chunks:
            for hd in heads:
                o_parts[c][hd].append(jnp.dot(sc[d][c][hd], v_bf[row_sl[c], col_sl[hd]],
                                              preferred_element_type=F32))
                if st_in[d][c][hd] is not None:
                    o_parts[c][hd].append(lax.dot_general(q_inter[d][c][:, col_sl[hd]],
                                                          st_in[d][c][hd], nt_dims,
                                                          preferred_element_type=F32))

    def emit(val, o1, o4, o16):
        o1[...] = val.astype(BF16)
        for hp, tmp in enumerate((tmp_lo, tmp_hi)):
            cols = slice(hp * LANES, (hp + 1) * LANES)
            tmp[...] = val[:, cols]
            for dil, o in ((4, o4), (16, o16)):
                for r in range(dil):
                    o[r, :, cols] = tmp[pl.ds(r, ts // dil, stride=dil), :].astype(BF16)

    cos_t = cos_ref[...]
    sin_t = sin_ref[...]
    scale = LOG2E * HEAD64 ** -0.5
    emit(_rope(_headnorm64(z_aq, aqw_ref[...]), cos_t, sin_t) * scale, q1_o, q4_o, q16_o)
    emit(_rope(_headnorm64(z_ak, akw_ref[...]), cos_t, sin_t), k1_o, k4_o, k16_o)
    emit(z_av, v1_o, v4_o, v16_o)
    mq_o[...] = (_headnorm64(z_mq, mqw_ref[...]) * scale).astype(BF16)
    for c in chunks:
        for hd in heads:
            o_sum = o_parts[c][hd][0]
            for part in o_parts[c][hd][1:]:
                o_sum = o_sum + part
            osum_o[row_sl[c], col_sl[hd]] = o_sum.astype(BF16)


def _front(x, nw, w_in, lbf, lbb, aqw, akw, mqw, cos_t, sin_t):
    B, S, _ = x.shape
    ts = PROJ_TILE
    nt = S // ts
    per_att = ATT_TILE // ts
    row = lambda n: pl.BlockSpec((None, ts, n), lambda b, i: (b, i, 0))
    vec = lambda n: pl.BlockSpec((1, n), lambda b, i: (0, 0))
    tab = pl.BlockSpec((ts, LANES), lambda b, i: (i, 0))

    def grouped(dil):
        spec = pl.BlockSpec((None, None, dil, ts // dil, AT_WIDTH),
                            lambda b, i: (b, i // per_att, 0, i % per_att, 0))
        shape = jax.ShapeDtypeStruct((B, S // ATT_TILE, dil, ATT_TILE // dil, AT_WIDTH), BF16)
        return spec, shape

    outs = [(row(HG_WIDTH), jax.ShapeDtypeStruct((B, S, HG_WIDTH), BF16)),
            (row(HG_WIDTH), jax.ShapeDtypeStruct((B, S, HG_WIDTH), BF16)),
            (pl.BlockSpec((None, None, HG_HEADS, HG_D, HG_D), lambda b, i: (b, i, 0, 0, 0)),
             jax.ShapeDtypeStruct((B, nt, HG_HEADS, HG_D, HG_D), F32)),
            (pl.BlockSpec((None, None, 1, HG_WIDTH), lambda b, i: (b, i, 0, 0)),
             jax.ShapeDtypeStruct((B, nt, 1, HG_WIDTH), F32))]
    for _ in range(3):
        outs.append((row(AT_WIDTH), jax.ShapeDtypeStruct((B, S, AT_WIDTH), BF16)))
        outs.append(grouped(4))
        outs.append(grouped(16))
    outs.append((row(MEM_WIDTH), jax.ShapeDtypeStruct((B, S, MEM_WIDTH), BF16)))
    return pl.pallas_call(
        _front_kernel,
        grid=(B, nt),
        in_specs=[row(D_MODEL), vec(D_MODEL),
                  pl.BlockSpec((D_MODEL, FRONT_WIDTH), lambda b, i: (0, 0)),
                  vec(HG_WIDTH), vec(HG_WIDTH), vec(AT_WIDTH), vec(AT_WIDTH), vec(MEM_WIDTH),
                  tab, tab],
        out_specs=[s for s, _ in outs],
        out_shape=[s for _, s in outs],
        scratch_shapes=[pltpu.VMEM((HG_HEADS, HG_D, HG_D), F32)]
                       + [pltpu.VMEM((ts, LANES), F32)] * 2,
        compiler_params=_cparams("parallel", "arbitrary"),
        name="front",
    )(x, nw, w_in, lbf, lbb, aqw, akw, mqw, cos_t, sin_t)


def _mem_kv_kernel(m_ref, nw_ref, w_ref, kw_ref, mk_o, mva_o, mvb_o):
    x = m_ref[...]
    ms = jnp.mean(x * x, axis=-1, keepdims=True)
    h = (x * lax.rsqrt(ms + NORM_EPS) * nw_ref[...]).astype(BF16)
    kv = jnp.dot(h, w_ref[...], preferred_element_type=F32)
    mk_o[...] = _headnorm64(kv[:, :MEM_WIDTH], kw_ref[...]).astype(BF16)
    lo_mask = (lax.broadcasted_iota(jnp.int32, (1, MEM_WIDTH), 1) % LANES) < HEAD64
    mv = kv[:, MEM_WIDTH:]
    mva_o[...] = jnp.where(lo_mask, mv, 1.0).astype(BF16)
    mvb_o[...] = jnp.where(lo_mask, 1.0, mv).astype(BF16)


def _mem_kv(mem, nw, wkv, kw):
    B, M, _ = mem.shape
    out = pl.BlockSpec((None, M, MEM_WIDTH), lambda b: (b, 0, 0))
    return pl.pallas_call(
        _mem_kv_kernel,
        grid=(B,),
        in_specs=[pl.BlockSpec((None, M, D_MODEL), lambda b: (b, 0, 0)),
                  pl.BlockSpec((1, D_MODEL), lambda b: (0, 0)),
                  pl.BlockSpec((D_MODEL, 2 * MEM_WIDTH), lambda b: (0, 0)),
                  pl.BlockSpec((1, MEM_WIDTH), lambda b: (0, 0))],
        out_specs=[out, out, out],
        out_shape=[jax.ShapeDtypeStruct((B, M, MEM_WIDTH), BF16)] * 3,
        compiler_params=_cparams("parallel"),
        name="mem_kv",
    )(mem, nw, wkv, kw)


def _attn_kernel(negm_ref, q1_ref, q4_ref, q16_ref,
                 k1p, k1m, k1n, k4p, k4m, k4n, k16p, k16m, k16n,
                 v1p, v1m, v1n, v4p, v4m, v4n, v16p, v16m, v16n,
                 o_ref, num_lo, num_hi, den_lo, den_hi, bias_s, *, nt):
    num_s = (num_lo, num_hi)
    den_s = (den_lo, den_hi)
    i = pl.program_id(1)
    lo_mask = _lane_iota() < HEAD64
    WIN = QBLK + 2 * BAND

    qi = lax.broadcasted_iota(jnp.int32, (QBLK, WIN), 0)
    ke = lax.broadcasted_iota(jnp.int32, (QBLK, WIN), 1)
    delta = ke - BAND - qi
    band = (delta <= BAND) & (delta >= -BAND)
    left_ok = (ke >= BAND) | (i > 0)
    right_ok = (ke < QBLK + BAND) | (i < nt - 1)
    for kind, ok in enumerate((band, band & left_ok, band & right_ok, band & left_ok & right_ok)):
        bias_s[kind] = jnp.where(ok, negm_ref[...], MASK_VALUE)

    def pattern(dil, q_ref, kp, km, kn, vp, vm, vn, assign):
        rows = ATT_TILE // dil
        blocks_per_res = rows // QBLK
        n_groups = dil * blocks_per_res // ATT_GROUP

        def window(p_ref, m_ref, n_ref, res, blk):
            a = blk * QBLK
            parts = [p_ref[res]] if blk == 0 else [m_ref[res, a - BAND:a, :]]
            parts.append(m_ref[res, a:a + QBLK, :])
            parts.append(n_ref[res] if blk == blocks_per_res - 1
                         else m_ref[res, a + QBLK:a + QBLK + BAND, :])
            return jnp.concatenate(parts, axis=0)

        def group(grp, res_of, blk_of):
            pairs, dests = [], []
            for u in range(ATT_GROUP):
                res, blk = res_of(grp, u), blk_of(u)
                a = blk * QBLK
                kind = (1 if blk == 0 else 0) + (2 if blk == blocks_per_res - 1 else 0)
                k_win = window(kp, km, kn, res, blk)
                v_win = window(vp, vm, vn, res, blk)
                out_rows = pl.ds(a, QBLK) if dil == 1 else pl.ds(a * dil + res, QBLK, stride=dil)
                for hp in range(AT_WIDTH // LANES):
                    cols = slice(hp * LANES, (hp + 1) * LANES)
                    v2 = v_win[:, cols]
                    one = jnp.ones_like(v2)
                    pairs.append((q_ref[res, a:a + QBLK, cols], k_win[:, cols],
                                  jnp.where(lo_mask, v2, one), jnp.where(lo_mask, one, v2),
                                  bias_s[kind]))
                    dests.append((hp, out_rows))
            for (hp, out_rows), (num, den) in zip(dests, _pairs_attend(pairs)):
                if assign:
                    num_s[hp][out_rows, :] = num
                    den_s[hp][out_rows, :] = den
                else:
                    num_s[hp][out_rows, :] += num
                    den_s[hp][out_rows, :] += den

        if blocks_per_res >= ATT_GROUP:
            per_res = blocks_per_res // ATT_GROUP
            for res in range(dil):
                for g in range(per_res):
                    group(0, lambda grp, u, res=res: res, lambda u, g=g: g * ATT_GROUP + u)
        else:
            res_per_group = ATT_GROUP // blocks_per_res

            def body(grp, carry):
                group(grp, lambda grp, u: grp * res_per_group + u // blocks_per_res,
                      lambda u: u % blocks_per_res)
                return carry

            lax.fori_loop(0, n_groups, body, 0)

    pattern(16, q16_ref, k16p, k16m, k16n, v16p, v16m, v16n, True)
    pattern(4, q4_ref, k4p, k4m, k4n, v4p, v4m, v4n, False)
    pattern(1, q1_ref, k1p, k1m, k1n, v1p, v1m, v1n, False)
    for hp in range(AT_WIDTH // LANES):
        o_ref[:, hp * LANES:(hp + 1) * LANES] = (num_s[hp][...] / den_s[hp][...]).astype(BF16)


def _attention(negm, q, k, v):
    B, S, _ = q[0].shape
    nt = S // ATT_TILE

    def view(t, dil):
        return t.reshape(B, nt, dil, ATT_TILE // dil, AT_WIDTH)

    def main(dil):
        return pl.BlockSpec((None, None, dil, ATT_TILE // dil, AT_WIDTH),
                            lambda b, i: (b, i, 0, 0, 0))

    def prev(dil):
        lastblk = ATT_TILE // dil // BAND - 1
        return pl.BlockSpec((None, None, dil, BAND, AT_WIDTH),
                            lambda b, i: (b, jnp.maximum(i - 1, 0), 0, lastblk, 0))

    def nxt(dil):
        return pl.BlockSpec((None, None, dil, BAND, AT_WIDTH),
                            lambda b, i: (b, jnp.minimum(i + 1, nt - 1), 0, 0, 0))

    qs = [view(t, d) for t, d in zip(q, DILATIONS)]
    ks = [view(t, d) for t, d in zip(k, DILATIONS)]
    vs = [view(t, d) for t, d in zip(v, DILATIONS)]
    halo_specs, halo_args = [], []
    for ts_ in (ks, vs):
        for t, d in zip(ts_, DILATIONS):
            halo_specs += [prev(d), main(d), nxt(d)]
            halo_args += [t, t, t]
    return pl.pallas_call(
        functools.partial(_attn_kernel, nt=nt),
        grid=(B, nt),
        in_specs=[pl.BlockSpec((1, QBLK + 2 * BAND), lambda b, i: (0, 0))]
                 + [main(d) for d in DILATIONS] + halo_specs,
        out_specs=pl.BlockSpec((None, ATT_TILE, AT_WIDTH), lambda b, i: (b, i, 0)),
        out_shape=jax.ShapeDtypeStruct((B, S, AT_WIDTH), BF16),
        scratch_shapes=[pltpu.VMEM((ATT_TILE, LANES), F32)] * 4
                       + [pltpu.VMEM((4, QBLK, QBLK + 2 * BAND), F32)],
        compiler_params=_cparams("parallel", "parallel"),
        name="dilated_attn",
    )(negm, *qs, *halo_args)


def _out_kernel(x_ref, nw_ref, wg_ref, osum_ref, qtb_ref, lst_ref, dtile_ref, oat_ref, mq_ref,
                mk_ref, mva_ref, mvb_ref, negm_ref, onw_ref, w_ref, y_ref, sin_ref):
    nt_dims = (((1,), (1,)), ((), ()))

    @pl.when(pl.program_id(1) == 0)
    def _():
        sin_ref[...] = jnp.zeros_like(sin_ref)

    x = x_ref[...]
    ms = jnp.mean(x * x, axis=-1, keepdims=True)
    h = (x * lax.rsqrt(ms + NORM_EPS) * nw_ref[...]).astype(BF16)
    z_g = jnp.dot(h, wg_ref[...], preferred_element_type=F32)
    mem = [_pair_attend(mq_ref[:, cols], mk_ref[:, cols], mva_ref[:, cols], mvb_ref[:, cols],
                        negm_ref[...])
           for cols in (slice(0, LANES), slice(LANES, 2 * LANES))]
    corr = [lax.dot_general(qtb_ref[:, hd * HG_D:(hd + 1) * HG_D], sin_ref[hd].astype(BF16),
                            nt_dims, preferred_element_type=F32) for hd in range(HG_HEADS)]
    dtile = dtile_ref[...]
    g = z_g * _sigmoid(z_g)
    parts = []
    for hd in range(HG_HEADS):
        cols = slice(hd * HG_D, (hd + 1) * HG_D)
        sin_ref[hd] = sin_ref[hd] * dtile[:, cols] + lst_ref[hd]
        t = osum_ref[:, cols].astype(F32) + corr[hd]
        ms = jnp.mean(t * t, axis=-1, keepdims=True)
        parts.append(t * lax.rsqrt(ms + NORM_EPS) * onw_ref[...] * g[:, cols])
    parts.append(oat_ref[...].astype(F32) * g[:, HG_WIDTH:HG_WIDTH + AT_WIDTH])
    for hp, (num, den) in enumerate(mem):
        gc = slice(HG_WIDTH + AT_WIDTH + hp * LANES, HG_WIDTH + AT_WIDTH + (hp + 1) * LANES)
        parts.append(num / den * g[:, gc])
    mixed = jnp.concatenate(parts, axis=-1).astype(BF16)
    y_ref[...] = x + jnp.dot(mixed, w_ref[...], preferred_element_type=F32)


def _out_stage(x, nw, w_in, o_sum, qtb, lst, dtile, o_at, mq, mk, mva, mvb, negm, onw, w_out):
    B, S, _ = x.shape
    ts = PROJ_TILE
    nt = S // ts
    row = lambda n: pl.BlockSpec((None, ts, n), lambda b, i: (b, nt - 1 - i, 0))
    memspec = pl.BlockSpec((None, MEM_TOKENS, MEM_WIDTH), lambda b, i: (b, 0, 0))
    return pl.pallas_call(
        _out_kernel,
        grid=(B, nt),
        in_specs=[row(D_MODEL), pl.BlockSpec((1, D_MODEL), lambda b, i: (0, 0)),
                  pl.BlockSpec((D_MODEL, IN_WIDTH - C_GATE),
                               lambda b, i: (0, C_GATE // (IN_WIDTH - C_GATE))),
                  row(HG_WIDTH), row(HG_WIDTH),
                  pl.BlockSpec((None, None, HG_HEADS, HG_D, HG_D),
                               lambda b, i: (b, nt - 1 - i, 0, 0, 0)),
                  pl.BlockSpec((None, None, 1, HG_WIDTH), lambda b, i: (b, nt - 1 - i, 0, 0)),
                  row(AT_WIDTH), row(MEM_WIDTH), memspec, memspec, memspec,
                  pl.BlockSpec((1, MEM_TOKENS), lambda b, i: (0, 0)),
                  pl.BlockSpec((1, HG_D), lambda b, i: (0, 0)),
                  pl.BlockSpec((D_MODEL, D_MODEL), lambda b, i: (0, 0))],
        out_specs=row(D_MODEL),
        out_shape=jax.ShapeDtypeStruct((B, S, D_MODEL), F32),
        scratch_shapes=[pltpu.VMEM((HG_HEADS, HG_D, HG_D), F32)],
        compiler_params=_cparams("parallel", "arbitrary"),
        name="out_stage",
    )(x, nw, w_in, o_sum, qtb, lst, dtile, o_at, mq, mk, mva, mvb, negm, onw, w_out)


def _rope_tables(S):
    half = ROPE_DIM // 2
    pos = jnp.arange(S, dtype=F32)
    inv_freq = ROPE_THETA ** (-jnp.arange(half, dtype=F32) * 2.0 / ROPE_DIM)
    ang = pos[:, None] * inv_freq[None, :]
    cos, sin = jnp.cos(ang), jnp.sin(ang)
    pad = HEAD64 - ROPE_DIM
    cos64 = jnp.concatenate([cos, cos, jnp.ones((S, pad), F32)], axis=-1)
    sin64 = jnp.concatenate([-sin, sin, jnp.zeros((S, pad), F32)], axis=-1)
    return jnp.tile(cos64, (1, 2)), jnp.tile(sin64, (1, 2))


def _lower_bounds(p):
    sm = jax.nn.softmax(p.astype(F32), axis=0)
    return jnp.cumsum(sm, axis=0) - sm[0:1]


def _tile4(w):
    return jnp.tile(w.astype(F32), (1, 4))


def _neg_stabiliser(wq, wk, width):
    bound = (HEAD64 ** 0.5 * LOG2E) * jnp.max(jnp.abs(wq), axis=-1) * jnp.max(jnp.abs(wk), axis=-1)
    bound = jnp.minimum(bound.astype(F32), EXP2_RANGE)
    return jnp.broadcast_to(-bound[:, None, None], (DEPTH, 1, width))


def _trunk(x, mem, prm):
    B, S, _ = x.shape
    assert S % ATT_TILE == 0 and ATT_TILE % PROJ_TILE == 0 and PROJ_TILE % SCAN_CHUNK == 0
    cos_t, sin_t = _rope_tables(S)
    for l in range(DEPTH):
        p = {k: v[l] for k, v in prm.items()}
        (o_sum, qtb, lst, dtile, q1, q4, q16, k1, k4, k16, v1, v4, v16, mq) = _front(
            x, p["norm_w"], p["w_in"], p["lbf"], p["lbb"], p["aqw"], p["akw"], p["mqw"],
            cos_t, sin_t)
        mk, mva, mvb = _mem_kv(mem, p["mem_norm_w"], p["mem_wkv"], p["mkw"])
        o_at = _attention(p["negm_at"], (q1, q4, q16), (k1, k4, k16), (v1, v4, v16))
        x = _out_stage(x, p["norm_w"], p["w_in"], o_sum, qtb, lst, dtile, o_at, mq, mk, mva, mvb,
                       p["negm_mem"], p["onw"], p["w_out"])
    return x


def kernel(x_prompt, x_sample, mem_prompt, mem_sample, norm_w, w_in, hgrn_lb_fwd, hgrn_lb_bwd,
           hgrn_onorm_w, attn_qnorm_w, attn_knorm_w, mem_norm_w, mem_wkv, mem_qnorm_w,
           mem_knorm_w, w_out):
    prm = {
        "norm_w": norm_w.astype(F32)[:, None, :],
        "w_in": w_in.astype(BF16),
        "lbf": _lower_bounds(hgrn_lb_fwd)[:, None, :],
        "lbb": _lower_bounds(hgrn_lb_bwd)[:, None, :],
        "onw": hgrn_onorm_w.astype(F32)[:, None, :],
        "aqw": _tile4(attn_qnorm_w)[:, None, :],
        "akw": _tile4(attn_knorm_w)[:, None, :],
        "mem_norm_w": mem_norm_w.astype(F32)[:, None, :],
        "mem_wkv": mem_wkv.astype(BF16),
        "mqw": _tile4(mem_qnorm_w)[:, None, :],
        "mkw": _tile4(mem_knorm_w)[:, None, :],
        "w_out": w_out.astype(BF16),
        "negm_at": _neg_stabiliser(attn_qnorm_w, attn_knorm_w, QBLK + 2 * BAND),
        "negm_mem": _neg_stabiliser(mem_qnorm_w, mem_knorm_w, MEM_TOKENS),
    }
    y_prompt = _trunk(x_prompt, mem_prompt, prm)
    y_sample = _trunk(x_sample, mem_sample, prm)
    return (y_prompt, y_sample)
```

```python
import functools

import jax
import jax.numpy as jnp
from jax import lax
from jax.experimental import pallas as pl
from jax.experimental.pallas import tpu as pltpu

F32 = jnp.float32
BF16 = jnp.bfloat16

D_MODEL = 1024
DEPTH = 4
HG_HEADS = 4
HG_D = 128
HG_WIDTH = HG_HEADS * HG_D
AT_WIDTH = 256
HEAD64 = 64
MEM_WIDTH = 256
MEM_TOKENS = 256
IN_WIDTH = 4096
DILATIONS = (1, 4, 16)
BAND = 64
ROPE_THETA = 500000.0
ROPE_DIM = 16
NORM_EPS = 1e-6
MASK_VALUE = -1e30
LANES = 128
SLAB_ROWS = 32

C_HQ, C_FF, C_FB, C_HI = 0, 512, 1024, 1536
C_AQ, C_AK, C_AV, C_MQ = 2048, 2304, 2560, 2816
C_GATE = 3072
FRONT_WIDTH = C_GATE

LOG2E = 1.4426950408889634
EXP2_RANGE = 63.0
PROJ_TILE = 512
ATT_TILE = 2048
QBLK = 128
ATT_GROUP = 8
SCAN_CHUNK = 64
EXP2_CLAMP = 115.0
VMEM_LIMIT = 56 * 1024 * 1024


def _cparams(*sem):
    return pltpu.CompilerParams(dimension_semantics=sem, vmem_limit_bytes=VMEM_LIMIT)


def _sigmoid(z):
    return 1.0 / (1.0 + jnp.exp(-z))


def _lane_iota(n=LANES):
    return lax.broadcasted_iota(jnp.int32, (1, n), 1)


def _slabs(fn, *xs, rows=SLAB_ROWS, width=LANES):
    n_rows = max(x.shape[0] for x in xs)
    n_cols = xs[0].shape[1]
    row_blocks = []
    for r0 in range(0, n_rows, rows):
        col_blocks = []
        for c0 in range(0, n_cols, width):
            res = fn(*[x[(slice(None) if x.shape[0] == 1 else slice(r0, r0 + rows)), c0:c0 + width]
                       for x in xs])
            col_blocks.append(res if isinstance(res, tuple) else (res,))
        row_blocks.append([jnp.concatenate(parts, axis=1) if len(parts) > 1 else parts[0]
                           for parts in zip(*col_blocks)])
    outs = [jnp.concatenate(parts, axis=0) if len(parts) > 1 else parts[0]
            for parts in zip(*row_blocks)]
    return outs[0] if len(outs) == 1 else tuple(outs)


def _rms_norm_bf16(x, w):
    def fn(xs, ws):
        ms = jnp.mean(xs * xs, axis=-1, keepdims=True)
        return (xs * lax.rsqrt(ms + NORM_EPS) * ws).astype(BF16)
    return _slabs(fn, x, w, width=x.shape[1])


def _silu(z):
    return _slabs(lambda t: t * _sigmoid(t), z)


def _headnorm64(t, w):
    lo_mask = _lane_iota() < HEAD64
    parts = []
    for hp in range(t.shape[1] // LANES):
        th = t[:, hp * LANES:(hp + 1) * LANES]
        t2 = th * th
        lo = jnp.sum(jnp.where(lo_mask, t2, 0.0), axis=-1, keepdims=True)
        hi = jnp.sum(jnp.where(lo_mask, 0.0, t2), axis=-1, keepdims=True)
        ms = jnp.where(lo_mask, lo, hi) * (1.0 / HEAD64)
        parts.append(th * lax.rsqrt(ms + NORM_EPS))
    return jnp.concatenate(parts, axis=-1) * w


def _rope(t, cos_t, sin_t):
    lane = _lane_iota() % HEAD64
    parts = []
    for hp in range(t.shape[1] // LANES):
        th = t[:, hp * LANES:(hp + 1) * LANES]
        partner = jnp.where(lane < ROPE_DIM // 2,
                            pltpu.roll(th, LANES - ROPE_DIM // 2, 1),
                            pltpu.roll(th, ROPE_DIM // 2, 1))
        parts.append(th * cos_t + partner * sin_t)
    return jnp.concatenate(parts, axis=-1)


def _pair_attend(q2, k2, va, vb, bias):
    return _pairs_attend([(q2, k2, va, vb, bias)])[0]


def _pairs_attend(pairs):
    lo_mask = _lane_iota() < HEAD64
    nt = (((1,), (1,)), ((), ()))
    s = []
    for q2, k2, _, _, _ in pairs:
        zero = jnp.zeros_like(q2)
        s.append((lax.dot_general(jnp.where(lo_mask, q2, zero), k2, nt, preferred_element_type=F32),
                  lax.dot_general(jnp.where(lo_mask, zero, q2), k2, nt, preferred_element_type=F32)))
    prob = lambda sc, bias: jnp.exp2(sc + bias).astype(BF16)
    p = [(_slabs(prob, s_lo, pr[4], width=s_lo.shape[1]), _slabs(prob, s_hi, pr[4], width=s_hi.shape[1]))
         for (s_lo, s_hi), pr in zip(s, pairs)]
    o = [(jnp.dot(p_lo, pr[2], preferred_element_type=F32),
          jnp.dot(p_hi, pr[3], preferred_element_type=F32)) for (p_lo, p_hi), pr in zip(p, pairs)]
    return [(jnp.where(lo_mask, o_lo, o_hi), pltpu.roll(jnp.where(lo_mask, o_hi, o_lo), HEAD64, 1))
            for o_lo, o_hi in o]


def _front_kernel(x_ref, nw_ref, w_ref, lbf_ref, lbb_ref, aqw_ref, akw_ref, mqw_ref,
                  cos_ref, sin_ref,
                  osum_o, qtb_o, lst_o, dtile_o,
                  q1_o, q4_o, q16_o, k1_o, k4_o, k16_o, v1_o, v4_o, v16_o, mq_o,
                  st_ref, tmp_lo, tmp_hi):
    ts = x_ref.shape[0]
    C = SCAN_CHUNK
    chunks = range(ts // C)
    heads = range(HG_HEADS)
    nt_dims = (((1,), (1,)), ((), ()))
    tn_dims = (((0,), (0,)), ((), ()))
    row_sl = [slice(c * C, (c + 1) * C) for c in chunks]
    col_sl = [slice(hd * HG_D, (hd + 1) * HG_D) for hd in heads]

    @pl.when(pl.program_id(1) == 0)
    def _():
        st_ref[...] = jnp.zeros_like(st_ref)

    h = _rms_norm_bf16(x_ref[...], nw_ref[...])

    def proj(c0, n):
        return jnp.dot(h, w_ref[:, c0:c0 + n], preferred_element_type=F32)

    def gate_terms(z, lb_ref):
        def fn(zs, lb):
            sig = _sigmoid(zs)
            lf = jnp.log2(lb + (1.0 - lb) * sig)
            lf_hi = lf.astype(BF16)
            return lf_hi, (lf - lf_hi.astype(F32)).astype(BF16), (1.0 - lb) * (1.0 - sig)
        return _slabs(fn, z, lb_ref[...])

    z_q = proj(C_HQ, HG_WIDTH)
    z_i = proj(C_HI, HG_WIDTH)
    q = _silu(z_q)
    z_f = proj(C_FF, HG_WIDTH)
    v_bf = z_i.astype(BF16)
    z_b = proj(C_FB, HG_WIDTH)
    lfh_f, lfl_f, kk_f = gate_terms(z_f, lbf_ref)
    z_aq = proj(C_AQ, AT_WIDTH)
    z_ak = proj(C_AK, AT_WIDTH)
    lfh_b, lfl_b, kk_b = gate_terms(z_b, lbb_ref)

    ri = lax.broadcasted_iota(jnp.int32, (C, C), 0)
    ci = lax.broadcasted_iota(jnp.int32, (C, C), 1)
    tri = [ci <= ri, ci >= ri]
    tri_bf = [jnp.tile(jnp.where(t, 1.0, 0.0).astype(BF16), (1, 2)) for t in tri]
    end_row = [C - 1, 0]
    mid_row = [C // 2 - 1, C // 2]
    lfh_d, lfl_d, kk_d = [lfh_f, lfh_b], [lfl_f, lfl_b], [kk_f, kk_b]
    dirs = range(2)

    b_all = [[None] * len(chunks) for _ in dirs]
    for d in dirs:
        for c in chunks:
            lf2 = jnp.concatenate([lfh_d[d][row_sl[c], :], lfl_d[d][row_sl[c], :]], axis=0)
            b_all[d][c] = jnp.dot(tri_bf[d], lf2, preferred_element_type=F32)
    z_av = proj(C_AV, AT_WIDTH)
    z_mq = proj(C_MQ, MEM_WIDTH)

    q_inter = [[None] * len(chunks) for _ in dirs]
    k_state = [[None] * len(chunks) for _ in dirs]
    q_intra = [[None] * len(chunks) for _ in dirs]
    k_intra = [[None] * len(chunks) for _ in dirs]
    dec = [[None] * len(chunks) for _ in dirs]
    for d in dirs:
        for c in chunks:
            b = b_all[d][c]
            b_end = b[end_row[d]:end_row[d] + 1, :]
            b_mid = b[mid_row[d]:mid_row[d] + 1, :]
            dec[d][c] = jnp.exp2(b_end)

            def operands(bs, qs, ks, be, bm):
                return ((qs * jnp.exp2(bs)).astype(BF16),
                        (ks * jnp.exp2(be - bs)).astype(BF16),
                        (qs * jnp.exp2(jnp.minimum(bs - bm, EXP2_CLAMP))).astype(BF16),
                        (ks * jnp.exp2(jnp.minimum(bm - bs, EXP2_CLAMP))).astype(BF16))

            q_inter[d][c], k_state[d][c], q_intra[d][c], k_intra[d][c] = _slabs(
                operands, b, q[row_sl[c], :], kk_d[d][row_sl[c], :], b_end, b_mid)
    off = jnp.zeros((1, HG_WIDTH), F32)
    for c in reversed(chunks):
        b = b_all[1][c]
        qtb_o[row_sl[c], :] = _slabs(lambda bs, qs, o: (qs * jnp.exp2(bs + o)).astype(BF16),
                                     b, q[row_sl[c], :], off)
        off = off + b[end_row[1]:end_row[1] + 1, :]
    dtile_o[...] = jnp.exp2(off)

    zeros_k = jnp.zeros((C, HG_D), BF16)

    def both_scores(c, hd):
        lhs = jnp.concatenate([q_intra[d][c][:, col_sl[hd]] for d in dirs], axis=1)
        rhs = jnp.concatenate(
            [jnp.concatenate([k_intra[0][c][:, col_sl[hd]], zeros_k], axis=1),
             jnp.concatenate([zeros_k, k_intra[1][c][:, col_sl[hd]]], axis=1)], axis=0)
        return lax.dot_general(lhs, rhs, nt_dims, preferred_element_type=F32)

    sc = [[both_scores(c, hd) for hd in heads] for c in chunks]
    upd2 = [[lax.dot_general(v_bf[row_sl[c], col_sl[hd]],
                             jnp.concatenate([k_state[d][c][:, col_sl[hd]] for d in dirs], axis=1),
                             tn_dims, preferred_element_type=F32) for hd in heads] for c in chunks]
    upd = [[[upd2[c][hd][:, d * HG_D:(d + 1) * HG_D] for hd in heads] for c in chunks]
           for d in dirs]
    ri2 = lax.broadcasted_iota(jnp.int32, (C, 2 * C), 0)
    ci2 = lax.broadcasted_iota(jnp.int32, (C, 2 * C), 1)
    tri2 = (ci2 <= ri2) | (ci2 - C >= ri2)
    sc = [[jnp.where(tri2, sc[c][hd], 0.0).astype(BF16) for hd in heads] for c in chunks]

    st_in = [[[None] * HG_HEADS for _ in chunks] for _ in dirs]
    for hd in heads:
        def forward(st, *terms):
            seen = []
            for c in chunks:
                seen.append(st.astype(BF16))
                st = st * terms[2 * c] + terms[2 * c + 1]
            return (st, *seen)

        def backward(*terms):
            seen, st = [], None
            for c in reversed(chunks):
                if st is None:
                    st = terms[2 * c + 1]
                else:
                    seen.append(st.astype(BF16))
                    st = st * terms[2 * c] + terms[2 * c + 1]
            return (st, *seen)

        terms = [[t for c in chunks for t in (dec[d][c][:, col_sl[hd]], upd[d][c][hd])]
                 for d in dirs]
        st_new, *seen = _slabs(forward, st_ref[hd], *terms[0])
        st_ref[hd] = st_new
        for c in chunks:
            st_in[0][c][hd] = seen[c]
        st_new, *seen = _slabs(backward, *terms[1])
        lst_o[hd] = st_new
        for j, c in enumerate(reversed(chunks[:-1])):
            st_in[1][c][hd] = seen[j]
    zeros_st = jnp.zeros((HG_D, HG_D), BF16)
    o_parts = [[[] for _ in heads] for _ in chunks]
    for c in chunks:
        for hd in heads:
            v_c = v_bf[row_sl[c], col_sl[hd]]
            o_parts[c][hd].append(jnp.dot(sc[c][hd], jnp.concatenate([v_c, v_c], axis=0),
                                          preferred_element_type=F32))
            states = [zeros_st if st_in[d][c][hd] is None else st_in[d][c][hd] for d in dirs]
            o_parts[c][hd].append(lax.dot_general(
                jnp.concatenate([q_inter[d][c][:, col_sl[hd]] for d in dirs], axis=1),
                jnp.concatenate(states, axis=1), nt_dims, preferred_element_type=F32))

    def emit(val, o1, o4, o16):
        o1[...] = val.astype(BF16)
        for hp, tmp in enumerate((tmp_lo, tmp_hi)):
            cols = slice(hp * LANES, (hp + 1) * LANES)
            tmp[...] = val[:, cols]
            for dil, o in ((4, o4), (16, o16)):
                for r in range(dil):
                    o[r, :, cols] = tmp[pl.ds(r, ts // dil, stride=dil), :].astype(BF16)

    cos_t = cos_ref[...]
    sin_t = sin_ref[...]
    scale = LOG2E * HEAD64 ** -0.5
    cos2, sin2 = jnp.tile(cos_t, (1, 2)), jnp.tile(sin_t, (1, 2))
    qk = lambda t, w, cs, sn: _rope(_headnorm64(t, w), cs[:, :LANES], sn[:, :LANES])
    emit(_slabs(lambda t, w, cs, sn: qk(t, w, cs, sn) * scale, z_aq, aqw_ref[...], cos2, sin2,
                width=AT_WIDTH), q1_o, q4_o, q16_o)
    emit(_slabs(qk, z_ak, akw_ref[...], cos2, sin2, width=AT_WIDTH), k1_o, k4_o, k16_o)
    emit(z_av, v1_o, v4_o, v16_o)
    mq_o[...] = _slabs(lambda t, w: (_headnorm64(t, w) * scale).astype(BF16), z_mq, mqw_ref[...],
                       width=MEM_WIDTH)
    for c in chunks:
        for hd in heads:
            o_sum = o_parts[c][hd][0]
            for part in o_parts[c][hd][1:]:
                o_sum = o_sum + part
            osum_o[row_sl[c], col_sl[hd]] = o_sum.astype(BF16)


def _front(x, nw, w_in, lbf, lbb, aqw, akw, mqw, cos_t, sin_t):
    B, S, _ = x.shape
    ts = PROJ_TILE
    nt = S // ts
    per_att = ATT_TILE // ts
    row = lambda n: pl.BlockSpec((None, ts, n), lambda b, i: (b, i, 0))
    vec = lambda n: pl.BlockSpec((1, n), lambda b, i: (0, 0))
    tab = pl.BlockSpec((ts, LANES), lambda b, i: (i, 0))

    def grouped(dil):
        spec = pl.BlockSpec((None, None, dil, ts // dil, AT_WIDTH),
                            lambda b, i: (b, i // per_att, 0, i % per_att, 0))
        shape = jax.ShapeDtypeStruct((B, S // ATT_TILE, dil, ATT_TILE // dil, AT_WIDTH), BF16)
        return spec, shape

    outs = [(row(HG_WIDTH), jax.ShapeDtypeStruct((B, S, HG_WIDTH), BF16)),
            (row(HG_WIDTH), jax.ShapeDtypeStruct((B, S, HG_WIDTH), BF16)),
            (pl.BlockSpec((None, None, HG_HEADS, HG_D, HG_D), lambda b, i: (b, i, 0, 0, 0)),
             jax.ShapeDtypeStruct((B, nt, HG_HEADS, HG_D, HG_D), F32)),
            (pl.BlockSpec((None, None, 1, HG_WIDTH), lambda b, i: (b, i, 0, 0)),
             jax.ShapeDtypeStruct((B, nt, 1, HG_WIDTH), F32))]
    for _ in range(3):
        outs.append((row(AT_WIDTH), jax.ShapeDtypeStruct((B, S, AT_WIDTH), BF16)))
        outs.append(grouped(4))
        outs.append(grouped(16))
    outs.append((row(MEM_WIDTH), jax.ShapeDtypeStruct((B, S, MEM_WIDTH), BF16)))
    return pl.pallas_call(
        _front_kernel,
        grid=(B, nt),
        in_specs=[row(D_MODEL), vec(D_MODEL),
                  pl.BlockSpec((D_MODEL, FRONT_WIDTH), lambda b, i: (0, 0)),
                  vec(HG_WIDTH), vec(HG_WIDTH), vec(AT_WIDTH), vec(AT_WIDTH), vec(MEM_WIDTH),
                  tab, tab],
        out_specs=[s for s, _ in outs],
        out_shape=[s for _, s in outs],
        scratch_shapes=[pltpu.VMEM((HG_HEADS, HG_D, HG_D), F32)]
                       + [pltpu.VMEM((ts, LANES), F32)] * 2,
        compiler_params=_cparams("parallel", "arbitrary"),
        name="front",
    )(x, nw, w_in, lbf, lbb, aqw, akw, mqw, cos_t, sin_t)


def _mem_kv_kernel(m_ref, nw_ref, w_ref, kw_ref, mk_o, mva_o, mvb_o):
    h = _rms_norm_bf16(m_ref[...], nw_ref[...])
    kv = jnp.dot(h, w_ref[...], preferred_element_type=F32)
    mk_o[...] = _headnorm64(kv[:, :MEM_WIDTH], kw_ref[...]).astype(BF16)
    lo_mask = (lax.broadcasted_iota(jnp.int32, (1, MEM_WIDTH), 1) % LANES) < HEAD64
    mv = kv[:, MEM_WIDTH:]
    mva_o[...] = jnp.where(lo_mask, mv, 1.0).astype(BF16)
    mvb_o[...] = jnp.where(lo_mask, 1.0, mv).astype(BF16)


def _mem_kv(mem, nw, wkv, kw):
    B, M, _ = mem.shape
    out = pl.BlockSpec((None, M, MEM_WIDTH), lambda b: (b, 0, 0))
    return pl.pallas_call(
        _mem_kv_kernel,
        grid=(B,),
        in_specs=[pl.BlockSpec((None, M, D_MODEL), lambda b: (b, 0, 0)),
                  pl.BlockSpec((1, D_MODEL), lambda b: (0, 0)),
                  pl.BlockSpec((D_MODEL, 2 * MEM_WIDTH), lambda b: (0, 0)),
                  pl.BlockSpec((1, MEM_WIDTH), lambda b: (0, 0))],
        out_specs=[out, out, out],
        out_shape=[jax.ShapeDtypeStruct((B, M, MEM_WIDTH), BF16)] * 3,
        compiler_params=_cparams("parallel"),
        name="mem_kv",
    )(mem, nw, wkv, kw)


def _attn_kernel(negm_ref, q1_ref, q4_ref, q16_ref,
                 k1p, k1m, k1n, k4p, k4m, k4n, k16p, k16m, k16n,
                 v1p, v1m, v1n, v4p, v4m, v4n, v16p, v16m, v16n,
                 o_ref, num_lo, num_hi, den_lo, den_hi, bias_s, *, nt):
    num_s = (num_lo, num_hi)
    den_s = (den_lo, den_hi)
    i = pl.program_id(1)
    lo_mask = _lane_iota() < HEAD64
    WIN = QBLK + 2 * BAND

    qi = lax.broadcasted_iota(jnp.int32, (QBLK, WIN), 0)
    ke = lax.broadcasted_iota(jnp.int32, (QBLK, WIN), 1)
    delta = ke - BAND - qi
    band = (delta <= BAND) & (delta >= -BAND)
    left_ok = (ke >= BAND) | (i > 0)
    right_ok = (ke < QBLK + BAND) | (i < nt - 1)
    for kind, ok in enumerate((band, band & left_ok, band & right_ok, band & left_ok & right_ok)):
        bias_s[kind] = jnp.where(ok, negm_ref[...], MASK_VALUE)

    def pattern(dil, q_ref, kp, km, kn, vp, vm, vn, assign):
        rows = ATT_TILE // dil
        blocks_per_res = rows // QBLK
        n_groups = dil * blocks_per_res // ATT_GROUP

        def window(p_ref, m_ref, n_ref, res, blk):
            a = blk * QBLK
            parts = [p_ref[res]] if blk == 0 else [m_ref[res, a - BAND:a, :]]
            parts.append(m_ref[res, a:a + QBLK, :])
            parts.append(n_ref[res] if blk == blocks_per_res - 1
                         else m_ref[res, a + QBLK:a + QBLK + BAND, :])
            return jnp.concatenate(parts, axis=0)

        def group(grp, res_of, blk_of):
            pairs, dests = [], []
            for u in range(ATT_GROUP):
                res, blk = res_of(grp, u), blk_of(u)
                a = blk * QBLK
                kind = (1 if blk == 0 else 0) + (2 if blk == blocks_per_res - 1 else 0)
                k_win = window(kp, km, kn, res, blk)
                v_win = window(vp, vm, vn, res, blk)
                out_rows = pl.ds(a, QBLK) if dil == 1 else pl.ds(a * dil + res, QBLK, stride=dil)
                for hp in range(AT_WIDTH // LANES):
                    cols = slice(hp * LANES, (hp + 1) * LANES)
                    v2 = v_win[:, cols]
                    one = jnp.ones_like(v2)
                    pairs.append((q_ref[res, a:a + QBLK, cols], k_win[:, cols],
                                  jnp.where(lo_mask, v2, one), jnp.where(lo_mask, one, v2),
                                  bias_s[kind]))
                    dests.append((hp, out_rows))
            for (hp, out_rows), (num, den) in zip(dests, _pairs_attend(pairs)):
                if assign:
                    num_s[hp][out_rows, :] = num
                    den_s[hp][out_rows, :] = den
                else:
                    num_s[hp][out_rows, :] += num
                    den_s[hp][out_rows, :] += den

        if blocks_per_res >= ATT_GROUP:
            per_res = blocks_per_res // ATT_GROUP
            for res in range(dil):
                for g in range(per_res):
                    group(0, lambda grp, u, res=res: res, lambda u, g=g: g * ATT_GROUP + u)
        else:
            res_per_group = ATT_GROUP // blocks_per_res

            def body(grp, carry):
                group(grp, lambda grp, u: grp * res_per_group + u // blocks_per_res,
                      lambda u: u % blocks_per_res)
                return carry

            lax.fori_loop(0, n_groups, body, 0)

    pattern(16, q16_ref, k16p, k16m, k16n, v16p, v16m, v16n, True)
    pattern(4, q4_ref, k4p, k4m, k4n, v4p, v4m, v4n, False)
    pattern(1, q1_ref, k1p, k1m, k1n, v1p, v1m, v1n, False)
    for hp in range(AT_WIDTH // LANES):
        o_ref[:, hp * LANES:(hp + 1) * LANES] = (num_s[hp][...] / den_s[hp][...]).astype(BF16)


def _attention(negm, q, k, v):
    B, S, _ = q[0].shape
    nt = S // ATT_TILE

    def view(t, dil):
        return t.reshape(B, nt, dil, ATT_TILE // dil, AT_WIDTH)

    def main(dil):
        return pl.BlockSpec((None, None, dil, ATT_TILE // dil, AT_WIDTH),
                            lambda b, i: (b, i, 0, 0, 0))

    def prev(dil):
        lastblk = ATT_TILE // dil // BAND - 1
        return pl.BlockSpec((None, None, dil, BAND, AT_WIDTH),
                            lambda b, i: (b, jnp.maximum(i - 1, 0), 0, lastblk, 0))

    def nxt(dil):
        return pl.BlockSpec((None, None, dil, BAND, AT_WIDTH),
                            lambda b, i: (b, jnp.minimum(i + 1, nt - 1), 0, 0, 0))

    qs = [view(t, d) for t, d in zip(q, DILATIONS)]
    ks = [view(t, d) for t, d in zip(k, DILATIONS)]
    vs = [view(t, d) for t, d in zip(v, DILATIONS)]
    halo_specs, halo_args = [], []
    for ts_ in (ks, vs):
        for t, d in zip(ts_, DILATIONS):
            halo_specs += [prev(d), main(d), nxt(d)]
            halo_args += [t, t, t]
    return pl.pallas_call(
        functools.partial(_attn_kernel, nt=nt),
        grid=(B, nt),
        in_specs=[pl.BlockSpec((1, QBLK + 2 * BAND), lambda b, i: (0, 0))]
                 + [main(d) for d in DILATIONS] + halo_specs,
        out_specs=pl.BlockSpec((None, ATT_TILE, AT_WIDTH), lambda b, i: (b, i, 0)),
        out_shape=jax.ShapeDtypeStruct((B, S, AT_WIDTH), BF16),
        scratch_shapes=[pltpu.VMEM((ATT_TILE, LANES), F32)] * 4
                       + [pltpu.VMEM((4, QBLK, QBLK + 2 * BAND), F32)],
        compiler_params=_cparams("parallel", "parallel"),
        name="dilated_attn",
    )(negm, *qs, *halo_args)


def _out_kernel(x_ref, nw_ref, wg_ref, osum_ref, qtb_ref, lst_ref, dtile_ref, oat_ref, mq_ref,
                mk_ref, mva_ref, mvb_ref, negm_ref, onw_ref, w_ref, y_ref, sin_ref):
    nt_dims = (((1,), (1,)), ((), ()))

    @pl.when(pl.program_id(1) == 0)
    def _():
        sin_ref[...] = jnp.zeros_like(sin_ref)

    h = _rms_norm_bf16(x_ref[...], nw_ref[...])
    z_g = jnp.dot(h, wg_ref[...], preferred_element_type=F32)
    mem = [_pair_attend(mq_ref[:, cols], mk_ref[:, cols], mva_ref[:, cols], mvb_ref[:, cols],
                        negm_ref[...])
           for cols in (slice(0, LANES), slice(LANES, 2 * LANES))]
    corr = [lax.dot_general(qtb_ref[:, hd * HG_D:(hd + 1) * HG_D], sin_ref[hd].astype(BF16),
                            nt_dims, preferred_element_type=F32) for hd in range(HG_HEADS)]
    dtile = dtile_ref[...]
    parts = []

    def hgrn_out(o_sum, cr, zg, w):
        t = o_sum.astype(F32) + cr
        ms = jnp.mean(t * t, axis=-1, keepdims=True)
        return (t * lax.rsqrt(ms + NORM_EPS) * w * (zg * _sigmoid(zg))).astype(BF16)

    for hd in range(HG_HEADS):
        cols = slice(hd * HG_D, (hd + 1) * HG_D)
        sin_ref[hd] = sin_ref[hd] * dtile[:, cols] + lst_ref[hd]
        parts.append(_slabs(hgrn_out, osum_ref[:, cols], corr[hd], z_g[:, cols], onw_ref[...]))
    c0 = HG_WIDTH
    parts.append(_slabs(lambda o, zg: (o.astype(F32) * (zg * _sigmoid(zg))).astype(BF16),
                        oat_ref[...], z_g[:, c0:c0 + AT_WIDTH]))
    c0 += AT_WIDTH
    for hp, (num, den) in enumerate(mem):
        parts.append(_slabs(lambda n, d, zg: (n / d * (zg * _sigmoid(zg))).astype(BF16),
                            num, den, z_g[:, c0 + hp * LANES:c0 + (hp + 1) * LANES]))
    mixed = jnp.concatenate(parts, axis=-1)
    y_ref[...] = x_ref[...] + jnp.dot(mixed, w_ref[...], preferred_element_type=F32)


def _out_stage(x, nw, w_in, o_sum, qtb, lst, dtile, o_at, mq, mk, mva, mvb, negm, onw, w_out):
    B, S, _ = x.shape
    ts = PROJ_TILE
    nt = S // ts
    row = lambda n: pl.BlockSpec((None, ts, n), lambda b, i: (b, nt - 1 - i, 0))
    memspec = pl.BlockSpec((None, MEM_TOKENS, MEM_WIDTH), lambda b, i: (b, 0, 0))
    return pl.pallas_call(
        _out_kernel,
        grid=(B, nt),
        in_specs=[row(D_MODEL), pl.BlockSpec((1, D_MODEL), lambda b, i: (0, 0)),
                  pl.BlockSpec((D_MODEL, IN_WIDTH - C_GATE),
                               lambda b, i: (0, C_GATE // (IN_WIDTH - C_GATE))),
                  row(HG_WIDTH), row(HG_WIDTH),
                  pl.BlockSpec((None, None, HG_HEADS, HG_D, HG_D),
                               lambda b, i: (b, nt - 1 - i, 0, 0, 0)),
                  pl.BlockSpec((None, None, 1, HG_WIDTH), lambda b, i: (b, nt - 1 - i, 0, 0)),
                  row(AT_WIDTH), row(MEM_WIDTH), memspec, memspec, memspec,
                  pl.BlockSpec((1, MEM_TOKENS), lambda b, i: (0, 0)),
                  pl.BlockSpec((1, HG_D), lambda b, i: (0, 0)),
                  pl.BlockSpec((D_MODEL, D_MODEL), lambda b, i: (0, 0))],
        out_specs=row(D_MODEL),
        out_shape=jax.ShapeDtypeStruct((B, S, D_MODEL), F32),
        scratch_shapes=[pltpu.VMEM((HG_HEADS, HG_D, HG_D), F32)],
        compiler_params=_cparams("parallel", "arbitrary"),
        name="out_stage",
    )(x, nw, w_in, o_sum, qtb, lst, dtile, o_at, mq, mk, mva, mvb, negm, onw, w_out)


def _rope_tables(S):
    half = ROPE_DIM // 2
    pos = jnp.arange(S, dtype=F32)
    inv_freq = ROPE_THETA ** (-jnp.arange(half, dtype=F32) * 2.0 / ROPE_DIM)
    ang = pos[:, None] * inv_freq[None, :]
    cos, sin = jnp.cos(ang), jnp.sin(ang)
    pad = HEAD64 - ROPE_DIM
    cos64 = jnp.concatenate([cos, cos, jnp.ones((S, pad), F32)], axis=-1)
    sin64 = jnp.concatenate([-sin, sin, jnp.zeros((S, pad), F32)], axis=-1)
    return jnp.tile(cos64, (1, 2)), jnp.tile(sin64, (1, 2))


def _lower_bounds(p):
    sm = jax.nn.softmax(p.astype(F32), axis=0)
    return jnp.cumsum(sm, axis=0) - sm[0:1]


def _tile4(w):
    return jnp.tile(w.astype(F32), (1, 4))


def _neg_stabiliser(wq, wk, width):
    bound = (HEAD64 ** 0.5 * LOG2E) * jnp.max(jnp.abs(wq), axis=-1) * jnp.max(jnp.abs(wk), axis=-1)
    bound = jnp.minimum(bound.astype(F32), EXP2_RANGE)
    return jnp.broadcast_to(-bound[:, None, None], (DEPTH, 1, width))


def _trunk(x, mem, prm):
    B, S, _ = x.shape
    assert S % ATT_TILE == 0 and ATT_TILE % PROJ_TILE == 0 and PROJ_TILE % SCAN_CHUNK == 0
    cos_t, sin_t = _rope_tables(S)
    for l in range(DEPTH):
        p = {k: v[l] for k, v in prm.items()}
        (o_sum, qtb, lst, dtile, q1, q4, q16, k1, k4, k16, v1, v4, v16, mq) = _front(
            x, p["norm_w"], p["w_in"], p["lbf"], p["lbb"], p["aqw"], p["akw"], p["mqw"],
            cos_t, sin_t)
        mk, mva, mvb = _mem_kv(mem, p["mem_norm_w"], p["mem_wkv"], p["mkw"])
        o_at = _attention(p["negm_at"], (q1, q4, q16), (k1, k4, k16), (v1, v4, v16))
        x = _out_stage(x, p["norm_w"], p["w_in"], o_sum, qtb, lst, dtile, o_at, mq, mk, mva, mvb,
                       p["negm_mem"], p["onw"], p["w_out"])
    return x


def kernel(x_prompt, x_sample, mem_prompt, mem_sample, norm_w, w_in, hgrn_lb_fwd, hgrn_lb_bwd,
           hgrn_onorm_w, attn_qnorm_w, attn_knorm_w, mem_norm_w, mem_wkv, mem_qnorm_w,
           mem_knorm_w, w_out):
    prm = {
        "norm_w": norm_w.astype(F32)[:, None, :],
        "w_in": w_in.astype(BF16),
        "lbf": _lower_bounds(hgrn_lb_fwd)[:, None, :],
        "lbb": _lower_bounds(hgrn_lb_bwd)[:, None, :],
        "onw": hgrn_onorm_w.astype(F32)[:, None, :],
        "aqw": _tile4(attn_qnorm_w)[:, None, :],
        "akw": _tile4(attn_knorm_w)[:, None, :],
        "mem_norm_w": mem_norm_w.astype(F32)[:, None, :],
        "mem_wkv": mem_wkv.astype(BF16),
        "mqw": _tile4(mem_qnorm_w)[:, None, :],
        "mkw": _tile4(mem_knorm_w)[:, None, :],
        "w_out": w_out.astype(BF16),
        "negm_at": _neg_stabiliser(attn_qnorm_w, attn_knorm_w, QBLK + 2 * BAND),
        "negm_mem": _neg_stabiliser(mem_qnorm_w, mem_knorm_w, MEM_TOKENS),
    }
    y_prompt = _trunk(x_prompt, mem_prompt, prm)
    y_sample = _trunk(x_sample, mem_sample, prm)
    return (y_prompt, y_sample)
```

```python
import functools

import jax
import jax.numpy as jnp
from jax import lax
from jax.experimental import pallas as pl
from jax.experimental.pallas import tpu as pltpu

F32 = jnp.float32
BF16 = jnp.bfloat16

D_MODEL = 1024
DEPTH = 4
HG_HEADS = 4
HG_D = 128
HG_WIDTH = HG_HEADS * HG_D
AT_WIDTH = 256
HEAD64 = 64
MEM_WIDTH = 256
MEM_TOKENS = 256
IN_WIDTH = 4096
DILATIONS = (1, 4, 16)
BAND = 64
ROPE_THETA = 500000.0
ROPE_DIM = 16
NORM_EPS = 1e-6
MASK_VALUE = -1e30
LANES = 128
SLAB_ROWS = 32

C_HQ, C_FF, C_FB, C_HI = 0, 512, 1024, 1536
C_AQ, C_AK, C_AV, C_MQ = 2048, 2304, 2560, 2816
C_GATE = 3072
FRONT_WIDTH = C_GATE

LOG2E = 1.4426950408889634
EXP2_RANGE = 63.0
PROJ_TILE = 512
ROW_BLK = 256
ATT_TILE = 2048
QBLK = 128
ATT_GROUP = 16
SCAN_CHUNK = 64
EXP2_CLAMP = 115.0
VMEM_LIMIT = 56 * 1024 * 1024


def _cparams(*sem):
    return pltpu.CompilerParams(dimension_semantics=sem, vmem_limit_bytes=VMEM_LIMIT)


def _sigmoid(z):
    return 1.0 / (1.0 + jnp.exp(-z))


def _lane_iota(n=LANES):
    return lax.broadcasted_iota(jnp.int32, (1, n), 1)


def _slabs(fn, *xs, rows=SLAB_ROWS, width=LANES):
    n_rows = max(x.shape[0] for x in xs)
    n_cols = xs[0].shape[1]
    row_blocks = []
    for r0 in range(0, n_rows, rows):
        col_blocks = []
        for c0 in range(0, n_cols, width):
            res = fn(*[x[(slice(None) if x.shape[0] == 1 else slice(r0, r0 + rows)), c0:c0 + width]
                       for x in xs])
            col_blocks.append(res if isinstance(res, tuple) else (res,))
        row_blocks.append([jnp.concatenate(parts, axis=1) if len(parts) > 1 else parts[0]
                           for parts in zip(*col_blocks)])
    outs = [jnp.concatenate(parts, axis=0) if len(parts) > 1 else parts[0]
            for parts in zip(*row_blocks)]
    return outs[0] if len(outs) == 1 else tuple(outs)


def _rms_norm_bf16(x, w):
    def fn(xs, ws):
        ms = jnp.mean(xs * xs, axis=-1, keepdims=True)
        return (xs * lax.rsqrt(ms + NORM_EPS) * ws).astype(BF16)
    return _slabs(fn, x, w, width=x.shape[1])


def _silu(z):
    return _slabs(lambda t: t * _sigmoid(t), z)


def _headnorm64(t, w):
    lo_mask = _lane_iota() < HEAD64
    parts = []
    for hp in range(t.shape[1] // LANES):
        th = t[:, hp * LANES:(hp + 1) * LANES]
        t2 = th * th
        lo = jnp.sum(jnp.where(lo_mask, t2, 0.0), axis=-1, keepdims=True)
        hi = jnp.sum(jnp.where(lo_mask, 0.0, t2), axis=-1, keepdims=True)
        ms = jnp.where(lo_mask, lo, hi) * (1.0 / HEAD64)
        parts.append(th * lax.rsqrt(ms + NORM_EPS))
    return jnp.concatenate(parts, axis=-1) * w


def _rope(t, cos_t, sin_t):
    lane = _lane_iota() % HEAD64
    parts = []
    for hp in range(t.shape[1] // LANES):
        th = t[:, hp * LANES:(hp + 1) * LANES]
        partner = jnp.where(lane < ROPE_DIM // 2,
                            pltpu.roll(th, LANES - ROPE_DIM // 2, 1),
                            pltpu.roll(th, ROPE_DIM // 2, 1))
        parts.append(th * cos_t + partner * sin_t)
    return jnp.concatenate(parts, axis=-1)


def _pair_attend(q2, k2, va, vb, bias):
    return _pairs_attend([(q2, k2, va, vb, bias)])[0]


def _pairs_attend(pairs):
    lo_mask = _lane_iota() < HEAD64
    nt = (((1,), (1,)), ((), ()))
    s = []
    for q2, k2, _, _, _ in pairs:
        zero = jnp.zeros_like(q2)
        s.append((lax.dot_general(jnp.where(lo_mask, q2, zero), k2, nt, preferred_element_type=F32),
                  lax.dot_general(jnp.where(lo_mask, zero, q2), k2, nt, preferred_element_type=F32)))
    prob = lambda sc, bias: jnp.exp2(sc + bias).astype(BF16)
    p = [(_slabs(prob, s_lo, pr[4], width=s_lo.shape[1]), _slabs(prob, s_hi, pr[4], width=s_hi.shape[1]))
         for (s_lo, s_hi), pr in zip(s, pairs)]
    o = [(jnp.dot(p_lo, pr[2], preferred_element_type=F32),
          jnp.dot(p_hi, pr[3], preferred_element_type=F32)) for (p_lo, p_hi), pr in zip(p, pairs)]
    return [(jnp.where(lo_mask, o_lo, o_hi), pltpu.roll(jnp.where(lo_mask, o_hi, o_lo), HEAD64, 1))
            for o_lo, o_hi in o]


def _front_kernel(x_ref, nw_ref, w_ref, lbf_ref, lbb_ref, aqw_ref, akw_ref, mqw_ref,
                  cos_ref, sin_ref,
                  osum_o, qtb_o, lst_o, dtile_o,
                  q1_o, q4_o, q16_o, k1_o, k4_o, k16_o, v1_o, v4_o, v16_o, mq_o,
                  st_ref, tmp_lo, tmp_hi):
    ts = x_ref.shape[0]
    C = SCAN_CHUNK
    chunks = range(ts // C)
    heads = range(HG_HEADS)
    nt_dims = (((1,), (1,)), ((), ()))
    tn_dims = (((0,), (0,)), ((), ()))
    row_sl = [slice(c * C, (c + 1) * C) for c in chunks]
    col_sl = [slice(hd * HG_D, (hd + 1) * HG_D) for hd in heads]

    @pl.when(pl.program_id(1) == 0)
    def _():
        st_ref[...] = jnp.zeros_like(st_ref)

    h = _rms_norm_bf16(x_ref[...], nw_ref[...])

    def proj(c0, n):
        return jnp.dot(h, w_ref[:, c0:c0 + n], preferred_element_type=F32)

    def gate_terms(z, lb_ref):
        def fn(zs, lb):
            sig = _sigmoid(zs)
            lf = jnp.log2(lb + (1.0 - lb) * sig)
            lf_hi = lf.astype(BF16)
            return lf_hi, (lf - lf_hi.astype(F32)).astype(BF16), (1.0 - lb) * (1.0 - sig)
        return _slabs(fn, z, lb_ref[...])

    def proj_rows(r0, c0, n):
        return jnp.dot(h[r0:r0 + ROW_BLK, :], w_ref[:, c0:c0 + n], preferred_element_type=F32)

    acc = {k: [] for k in ("q", "v", "ff", "fb", "aq", "ak")}
    for r0 in range(0, ts, ROW_BLK):
        z_q = proj_rows(r0, C_HQ, HG_WIDTH)
        z_i = proj_rows(r0, C_HI, HG_WIDTH)
        acc["q"].append(_silu(z_q))
        z_f = proj_rows(r0, C_FF, HG_WIDTH)
        acc["v"].append(z_i.astype(BF16))
        z_b = proj_rows(r0, C_FB, HG_WIDTH)
        acc["ff"].append(gate_terms(z_f, lbf_ref))
        acc["aq"].append(proj_rows(r0, C_AQ, AT_WIDTH))
        acc["ak"].append(proj_rows(r0, C_AK, AT_WIDTH))
        acc["fb"].append(gate_terms(z_b, lbb_ref))
    cat = lambda parts: jnp.concatenate(parts, axis=0)
    q, v_bf, z_aq, z_ak = cat(acc["q"]), cat(acc["v"]), cat(acc["aq"]), cat(acc["ak"])
    lfh_f, lfl_f, kk_f = (cat(parts) for parts in zip(*acc["ff"]))
    lfh_b, lfl_b, kk_b = (cat(parts) for parts in zip(*acc["fb"]))

    ri = lax.broadcasted_iota(jnp.int32, (C, C), 0)
    ci = lax.broadcasted_iota(jnp.int32, (C, C), 1)
    tri = [ci <= ri, ci >= ri]
    tri_bf = [jnp.tile(jnp.where(t, 1.0, 0.0).astype(BF16), (1, 2)) for t in tri]
    end_row = [C - 1, 0]
    mid_row = [C // 2 - 1, C // 2]
    lfh_d, lfl_d, kk_d = [lfh_f, lfh_b], [lfl_f, lfl_b], [kk_f, kk_b]
    dirs = range(2)

    b_all = [[None] * len(chunks) for _ in dirs]
    for d in dirs:
        for c in chunks:
            lf2 = jnp.concatenate([lfh_d[d][row_sl[c], :], lfl_d[d][row_sl[c], :]], axis=0)
            b_all[d][c] = jnp.dot(tri_bf[d], lf2, preferred_element_type=F32)
    z_av = proj(C_AV, AT_WIDTH)
    z_mq = proj(C_MQ, MEM_WIDTH)

    q_inter = [[None] * len(chunks) for _ in dirs]
    k_state = [[None] * len(chunks) for _ in dirs]
    q_intra = [[None] * len(chunks) for _ in dirs]
    k_intra = [[None] * len(chunks) for _ in dirs]
    dec = [[None] * len(chunks) for _ in dirs]
    for d in dirs:
        for c in chunks:
            b = b_all[d][c]
            b_end = b[end_row[d]:end_row[d] + 1, :]
            b_mid = b[mid_row[d]:mid_row[d] + 1, :]
            dec[d][c] = jnp.exp2(b_end)

            def operands(bs, qs, ks, be, bm):
                return ((qs * jnp.exp2(bs)).astype(BF16),
                        (ks * jnp.exp2(be - bs)).astype(BF16),
                        (qs * jnp.exp2(jnp.minimum(bs - bm, EXP2_CLAMP))).astype(BF16),
                        (ks * jnp.exp2(jnp.minimum(bm - bs, EXP2_CLAMP))).astype(BF16))

            q_inter[d][c], k_state[d][c], q_intra[d][c], k_intra[d][c] = _slabs(
                operands, b, q[row_sl[c], :], kk_d[d][row_sl[c], :], b_end, b_mid)
    off = jnp.zeros((1, HG_WIDTH), F32)
    for c in reversed(chunks):
        b = b_all[1][c]
        qtb_o[row_sl[c], :] = _slabs(lambda bs, qs, o: (qs * jnp.exp2(bs + o)).astype(BF16),
                                     b, q[row_sl[c], :], off)
        off = off + b[end_row[1]:end_row[1] + 1, :]
    dtile_o[...] = jnp.exp2(off)

    zeros_k = jnp.zeros((C, HG_D), BF16)

    def both_scores(c, hd):
        lhs = jnp.concatenate([q_intra[d][c][:, col_sl[hd]] for d in dirs], axis=1)
        rhs = jnp.concatenate(
            [jnp.concatenate([k_intra[0][c][:, col_sl[hd]], zeros_k], axis=1),
             jnp.concatenate([zeros_k, k_intra[1][c][:, col_sl[hd]]], axis=1)], axis=0)
        return lax.dot_general(lhs, rhs, nt_dims, preferred_element_type=F32)

    sc = [[both_scores(c, hd) for hd in heads] for c in chunks]
    upd2 = [[lax.dot_general(v_bf[row_sl[c], col_sl[hd]],
                             jnp.concatenate([k_state[d][c][:, col_sl[hd]] for d in dirs], axis=1),
                             tn_dims, preferred_element_type=F32) for hd in heads] for c in chunks]
    upd = [[[upd2[c][hd][:, d * HG_D:(d + 1) * HG_D] for hd in heads] for c in chunks]
           for d in dirs]
    ri2 = lax.broadcasted_iota(jnp.int32, (C, 2 * C), 0)
    ci2 = lax.broadcasted_iota(jnp.int32, (C, 2 * C), 1)
    tri2 = (ci2 <= ri2) | (ci2 - C >= ri2)
    sc = [[jnp.where(tri2, sc[c][hd], 0.0).astype(BF16) for hd in heads] for c in chunks]

    st_in = [[[None] * HG_HEADS for _ in chunks] for _ in dirs]
    for hd in heads:
        def forward(st, *terms):
            seen = []
            for c in chunks:
                seen.append(st.astype(BF16))
                st = st * terms[2 * c] + terms[2 * c + 1]
            return (st, *seen)

        def backward(*terms):
            seen, st = [], None
            for c in reversed(chunks):
                if st is None:
                    st = terms[2 * c + 1]
                else:
                    seen.append(st.astype(BF16))
                    st = st * terms[2 * c] + terms[2 * c + 1]
            return (st, *seen)

        terms = [[t for c in chunks for t in (dec[d][c][:, col_sl[hd]], upd[d][c][hd])]
                 for d in dirs]
        st_new, *seen = _slabs(forward, st_ref[hd], *terms[0])
        st_ref[hd] = st_new
        for c in chunks:
            st_in[0][c][hd] = seen[c]
        st_new, *seen = _slabs(backward, *terms[1])
        lst_o[hd] = st_new
        for j, c in enumerate(reversed(chunks[:-1])):
            st_in[1][c][hd] = seen[j]
    zeros_st = jnp.zeros((HG_D, HG_D), BF16)
    o_parts = [[[] for _ in heads] for _ in chunks]
    for c in chunks:
        for hd in heads:
            v_c = v_bf[row_sl[c], col_sl[hd]]
            o_parts[c][hd].append(jnp.dot(sc[c][hd], jnp.concatenate([v_c, v_c], axis=0),
                                          preferred_element_type=F32))
            states = [zeros_st if st_in[d][c][hd] is None else st_in[d][c][hd] for d in dirs]
            o_parts[c][hd].append(lax.dot_general(
                jnp.concatenate([q_inter[d][c][:, col_sl[hd]] for d in dirs], axis=1),
                jnp.concatenate(states, axis=1), nt_dims, preferred_element_type=F32))

    def emit(val, o1, o4, o16):
        o1[...] = val.astype(BF16)
        for hp, tmp in enumerate((tmp_lo, tmp_hi)):
            cols = slice(hp * LANES, (hp + 1) * LANES)
            tmp[...] = val[:, cols]
            for dil, o in ((4, o4), (16, o16)):
                for r in range(dil):
                    o[r, :, cols] = tmp[pl.ds(r, ts // dil, stride=dil), :].astype(BF16)

    cos_t = cos_ref[...]
    sin_t = sin_ref[...]
    scale = LOG2E * HEAD64 ** -0.5
    cos2, sin2 = jnp.tile(cos_t, (1, 2)), jnp.tile(sin_t, (1, 2))
    qk = lambda t, w, cs, sn: _rope(_headnorm64(t, w), cs[:, :LANES], sn[:, :LANES])
    emit(_slabs(lambda t, w, cs, sn: qk(t, w, cs, sn) * scale, z_aq, aqw_ref[...], cos2, sin2,
                width=AT_WIDTH), q1_o, q4_o, q16_o)
    emit(_slabs(qk, z_ak, akw_ref[...], cos2, sin2, width=AT_WIDTH), k1_o, k4_o, k16_o)
    emit(z_av, v1_o, v4_o, v16_o)
    mq_o[...] = _slabs(lambda t, w: (_headnorm64(t, w) * scale).astype(BF16), z_mq, mqw_ref[...],
                       width=MEM_WIDTH)
    for c in chunks:
        for hd in heads:
            o_sum = o_parts[c][hd][0]
            for part in o_parts[c][hd][1:]:
                o_sum = o_sum + part
            osum_o[row_sl[c], col_sl[hd]] = o_sum.astype(BF16)


def _front(x, nw, w_in, lbf, lbb, aqw, akw, mqw, cos_t, sin_t):
    B, S, _ = x.shape
    ts = PROJ_TILE
    nt = S // ts
    per_att = ATT_TILE // ts
    row = lambda n: pl.BlockSpec((None, ts, n), lambda b, i: (b, i, 0))
    vec = lambda n: pl.BlockSpec((1, n), lambda b, i: (0, 0))
    tab = pl.BlockSpec((ts, LANES), lambda b, i: (i, 0))

    def grouped(dil):
        spec = pl.BlockSpec((None, None, dil, ts // dil, AT_WIDTH),
                            lambda b, i: (b, i // per_att, 0, i % per_att, 0))
        shape = jax.ShapeDtypeStruct((B, S // ATT_TILE, dil, ATT_TILE // dil, AT_WIDTH), BF16)
        return spec, shape

    outs = [(row(HG_WIDTH), jax.ShapeDtypeStruct((B, S, HG_WIDTH), BF16)),
            (row(HG_WIDTH), jax.ShapeDtypeStruct((B, S, HG_WIDTH), BF16)),
            (pl.BlockSpec((None, None, HG_HEADS, HG_D, HG_D), lambda b, i: (b, i, 0, 0, 0)),
             jax.ShapeDtypeStruct((B, nt, HG_HEADS, HG_D, HG_D), F32)),
            (pl.BlockSpec((None, None, 1, HG_WIDTH), lambda b, i: (b, i, 0, 0)),
             jax.ShapeDtypeStruct((B, nt, 1, HG_WIDTH), F32))]
    for _ in range(3):
        outs.append((row(AT_WIDTH), jax.ShapeDtypeStruct((B, S, AT_WIDTH), BF16)))
        outs.append(grouped(4))
        outs.append(grouped(16))
    outs.append((row(MEM_WIDTH), jax.ShapeDtypeStruct((B, S, MEM_WIDTH), BF16)))
    return pl.pallas_call(
        _front_kernel,
        grid=(B, nt),
        in_specs=[row(D_MODEL), vec(D_MODEL),
                  pl.BlockSpec((D_MODEL, FRONT_WIDTH), lambda b, i: (0, 0)),
                  vec(HG_WIDTH), vec(HG_WIDTH), vec(AT_WIDTH), vec(AT_WIDTH), vec(MEM_WIDTH),
                  tab, tab],
        out_specs=[s for s, _ in outs],
        out_shape=[s for _, s in outs],
        scratch_shapes=[pltpu.VMEM((HG_HEADS, HG_D, HG_D), F32)]
                       + [pltpu.VMEM((ts, LANES), F32)] * 2,
        compiler_params=_cparams("parallel", "arbitrary"),
        name="front",
    )(x, nw, w_in, lbf, lbb, aqw, akw, mqw, cos_t, sin_t)


def _mem_kv_kernel(m_ref, nw_ref, w_ref, kw_ref, mk_o, mva_o, mvb_o):
    h = _rms_norm_bf16(m_ref[...], nw_ref[...])
    kv = jnp.dot(h, w_ref[...], preferred_element_type=F32)
    mk_o[...] = _headnorm64(kv[:, :MEM_WIDTH], kw_ref[...]).astype(BF16)
    lo_mask = (lax.broadcasted_iota(jnp.int32, (1, MEM_WIDTH), 1) % LANES) < HEAD64
    mv = kv[:, MEM_WIDTH:]
    mva_o[...] = jnp.where(lo_mask, mv, 1.0).astype(BF16)
    mvb_o[...] = jnp.where(lo_mask, 1.0, mv).astype(BF16)


def _mem_kv(mem, nw, wkv, kw):
    B, M, _ = mem.shape
    out = pl.BlockSpec((None, M, MEM_WIDTH), lambda b: (b, 0, 0))
    return pl.pallas_call(
        _mem_kv_kernel,
        grid=(B,),
        in_specs=[pl.BlockSpec((None, M, D_MODEL), lambda b: (b, 0, 0)),
                  pl.BlockSpec((1, D_MODEL), lambda b: (0, 0)),
                  pl.BlockSpec((D_MODEL, 2 * MEM_WIDTH), lambda b: (0, 0)),
                  pl.BlockSpec((1, MEM_WIDTH), lambda b: (0, 0))],
        out_specs=[out, out, out],
        out_shape=[jax.ShapeDtypeStruct((B, M, MEM_WIDTH), BF16)] * 3,
        compiler_params=_cparams("parallel"),
        name="mem_kv",
    )(mem, nw, wkv, kw)


def _attn_kernel(negm_ref, q1_ref, q4_ref, q16_ref,
                 k1p, k1m, k1n, k4p, k4m, k4n, k16p, k16m, k16n,
                 v1p, v1m, v1n, v4p, v4m, v4n, v16p, v16m, v16n,
                 o_ref, num_lo, num_hi, den_lo, den_hi, bias_s, *, nt):
    num_s = (num_lo, num_hi)
    den_s = (den_lo, den_hi)
    i = pl.program_id(1)
    lo_mask = _lane_iota() < HEAD64
    WIN = QBLK + 2 * BAND

    qi = lax.broadcasted_iota(jnp.int32, (QBLK, WIN), 0)
    ke = lax.broadcasted_iota(jnp.int32, (QBLK, WIN), 1)
    delta = ke - BAND - qi
    band = (delta <= BAND) & (delta >= -BAND)
    left_ok = (ke >= BAND) | (i > 0)
    right_ok = (ke < QBLK + BAND) | (i < nt - 1)
    for kind, ok in enumerate((band, band & left_ok, band & right_ok, band & left_ok & right_ok)):
        bias_s[kind] = jnp.where(ok, negm_ref[...], MASK_VALUE)

    def pattern(dil, q_ref, kp, km, kn, vp, vm, vn, assign):
        rows = ATT_TILE // dil
        blocks_per_res = rows // QBLK
        n_groups = dil * blocks_per_res // ATT_GROUP

        def window(p_ref, m_ref, n_ref, res, blk):
            a = blk * QBLK
            parts = [p_ref[res]] if blk == 0 else [m_ref[res, a - BAND:a, :]]
            parts.append(m_ref[res, a:a + QBLK, :])
            parts.append(n_ref[res] if blk == blocks_per_res - 1
                         else m_ref[res, a + QBLK:a + QBLK + BAND, :])
            return jnp.concatenate(parts, axis=0)

        def group(grp, res_of, blk_of):
            pairs, dests = [], []
            for u in range(ATT_GROUP):
                res, blk = res_of(grp, u), blk_of(u)
                a = blk * QBLK
                kind = (1 if blk == 0 else 0) + (2 if blk == blocks_per_res - 1 else 0)
                k_win = window(kp, km, kn, res, blk)
                v_win = window(vp, vm, vn, res, blk)
                out_rows = pl.ds(a, QBLK) if dil == 1 else pl.ds(a * dil + res, QBLK, stride=dil)
                for hp in range(AT_WIDTH // LANES):
                    cols = slice(hp * LANES, (hp + 1) * LANES)
                    v2 = v_win[:, cols]
                    one = jnp.ones_like(v2)
                    pairs.append((q_ref[res, a:a + QBLK, cols], k_win[:, cols],
                                  jnp.where(lo_mask, v2, one), jnp.where(lo_mask, one, v2),
                                  bias_s[kind]))
                    dests.append((hp, out_rows))
            for (hp, out_rows), (num, den) in zip(dests, _pairs_attend(pairs)):
                if assign:
                    num_s[hp][out_rows, :] = num
                    den_s[hp][out_rows, :] = den
                else:
                    num_s[hp][out_rows, :] += num
                    den_s[hp][out_rows, :] += den

        if blocks_per_res >= ATT_GROUP:
            per_res = blocks_per_res // ATT_GROUP
            for res in range(dil):
                for g in range(per_res):
                    group(0, lambda grp, u, res=res: res, lambda u, g=g: g * ATT_GROUP + u)
        else:
            res_per_group = ATT_GROUP // blocks_per_res

            def body(grp, carry):
                group(grp, lambda grp, u: grp * res_per_group + u // blocks_per_res,
                      lambda u: u % blocks_per_res)
                return carry

            lax.fori_loop(0, n_groups, body, 0)

    pattern(16, q16_ref, k16p, k16m, k16n, v16p, v16m, v16n, True)
    pattern(4, q4_ref, k4p, k4m, k4n, v4p, v4m, v4n, False)
    pattern(1, q1_ref, k1p, k1m, k1n, v1p, v1m, v1n, False)
    for hp in range(AT_WIDTH // LANES):
        o_ref[:, hp * LANES:(hp + 1) * LANES] = (num_s[hp][...] / den_s[hp][...]).astype(BF16)


def _attention(negm, q, k, v):
    B, S, _ = q[0].shape
    nt = S // ATT_TILE

    def view(t, dil):
        return t.reshape(B, nt, dil, ATT_TILE // dil, AT_WIDTH)

    def main(dil):
        return pl.BlockSpec((None, None, dil, ATT_TILE // dil, AT_WIDTH),
                            lambda b, i: (b, i, 0, 0, 0))

    def prev(dil):
        lastblk = ATT_TILE // dil // BAND - 1
        return pl.BlockSpec((None, None, dil, BAND, AT_WIDTH),
                            lambda b, i: (b, jnp.maximum(i - 1, 0), 0, lastblk, 0))

    def nxt(dil):
        return pl.BlockSpec((None, None, dil, BAND, AT_WIDTH),
                            lambda b, i: (b, jnp.minimum(i + 1, nt - 1), 0, 0, 0))

    qs = [view(t, d) for t, d in zip(q, DILATIONS)]
    ks = [view(t, d) for t, d in zip(k, DILATIONS)]
    vs = [view(t, d) for t, d in zip(v, DILATIONS)]
    halo_specs, halo_args = [], []
    for ts_ in (ks, vs):
        for t, d in zip(ts_, DILATIONS):
            halo_specs += [prev(d), main(d), nxt(d)]
            halo_args += [t, t, t]
    return pl.pallas_call(
        functools.partial(_attn_kernel, nt=nt),
        grid=(B, nt),
        in_specs=[pl.BlockSpec((1, QBLK + 2 * BAND), lambda b, i: (0, 0))]
                 + [main(d) for d in DILATIONS] + halo_specs,
        out_specs=pl.BlockSpec((None, ATT_TILE, AT_WIDTH), lambda b, i: (b, i, 0)),
        out_shape=jax.ShapeDtypeStruct((B, S, AT_WIDTH), BF16),
        scratch_shapes=[pltpu.VMEM((ATT_TILE, LANES), F32)] * 4
                       + [pltpu.VMEM((4, QBLK, QBLK + 2 * BAND), F32)],
        compiler_params=_cparams("parallel", "parallel"),
        name="dilated_attn",
    )(negm, *qs, *halo_args)


def _out_kernel(x_ref, nw_ref, wg_ref, osum_ref, qtb_ref, lst_ref, dtile_ref, oat_ref, mq_ref,
                mk_ref, mva_ref, mvb_ref, negm_ref, onw_ref, w_ref, y_ref, sin_ref):
    nt_dims = (((1,), (1,)), ((), ()))

    @pl.when(pl.program_id(1) == 0)
    def _():
        sin_ref[...] = jnp.zeros_like(sin_ref)

    h = _rms_norm_bf16(x_ref[...], nw_ref[...])
    z_g = jnp.dot(h, wg_ref[...], preferred_element_type=F32)
    mem = [_pair_attend(mq_ref[:, cols], mk_ref[:, cols], mva_ref[:, cols], mvb_ref[:, cols],
                        negm_ref[...])
           for cols in (slice(0, LANES), slice(LANES, 2 * LANES))]
    corr = [lax.dot_general(qtb_ref[:, hd * HG_D:(hd + 1) * HG_D], sin_ref[hd].astype(BF16),
                            nt_dims, preferred_element_type=F32) for hd in range(HG_HEADS)]
    dtile = dtile_ref[...]
    parts = []

    def hgrn_out(o_sum, cr, zg, w):
        t = o_sum.astype(F32) + cr
        ms = jnp.mean(t * t, axis=-1, keepdims=True)
        return (t * lax.rsqrt(ms + NORM_EPS) * w * (zg * _sigmoid(zg))).astype(BF16)

    for hd in range(HG_HEADS):
        cols = slice(hd * HG_D, (hd + 1) * HG_D)
        sin_ref[hd] = sin_ref[hd] * dtile[:, cols] + lst_ref[hd]
        parts.append(_slabs(hgrn_out, osum_ref[:, cols], corr[hd], z_g[:, cols], onw_ref[...]))
    c0 = HG_WIDTH
    parts.append(_slabs(lambda o, zg: (o.astype(F32) * (zg * _sigmoid(zg))).astype(BF16),
                        oat_ref[...], z_g[:, c0:c0 + AT_WIDTH]))
    c0 += AT_WIDTH
    for hp, (num, den) in enumerate(mem):
        parts.append(_slabs(lambda n, d, zg: (n / d * (zg * _sigmoid(zg))).astype(BF16),
                            num, den, z_g[:, c0 + hp * LANES:c0 + (hp + 1) * LANES]))
    mixed = jnp.concatenate(parts, axis=-1)
    y_ref[...] = x_ref[...] + jnp.dot(mixed, w_ref[...], preferred_element_type=F32)


def _out_stage(x, nw, w_in, o_sum, qtb, lst, dtile, o_at, mq, mk, mva, mvb, negm, onw, w_out):
    B, S, _ = x.shape
    ts = PROJ_TILE
    nt = S // ts
    row = lambda n: pl.BlockSpec((None, ts, n), lambda b, i: (b, nt - 1 - i, 0))
    memspec = pl.BlockSpec((None, MEM_TOKENS, MEM_WIDTH), lambda b, i: (b, 0, 0))
    return pl.pallas_call(
        _out_kernel,
        grid=(B, nt),
        in_specs=[row(D_MODEL), pl.BlockSpec((1, D_MODEL), lambda b, i: (0, 0)),
                  pl.BlockSpec((D_MODEL, IN_WIDTH - C_GATE),
                               lambda b, i: (0, C_GATE // (IN_WIDTH - C_GATE))),
                  row(HG_WIDTH), row(HG_WIDTH),
                  pl.BlockSpec((None, None, HG_HEADS, HG_D, HG_D),
                               lambda b, i: (b, nt - 1 - i, 0, 0, 0)),
                  pl.BlockSpec((None, None, 1, HG_WIDTH), lambda b, i: (b, nt - 1 - i, 0, 0)),
                  row(AT_WIDTH), row(MEM_WIDTH), memspec, memspec, memspec,
                  pl.BlockSpec((1, MEM_TOKENS), lambda b, i: (0, 0)),
                  pl.BlockSpec((1, HG_D), lambda b, i: (0, 0)),
                  pl.BlockSpec((D_MODEL, D_MODEL), lambda b, i: (0, 0))],
        out_specs=row(D_MODEL),
        out_shape=jax.ShapeDtypeStruct((B, S, D_MODEL), F32),
        scratch_shapes=[pltpu.VMEM((HG_HEADS, HG_D, HG_D), F32)],
        compiler_params=_cparams("parallel", "arbitrary"),
        name="out_stage",
    )(x, nw, w_in, o_sum, qtb, lst, dtile, o_at, mq, mk, mva, mvb, negm, onw, w_out)


def _rope_tables(S):
    half = ROPE_DIM // 2
    pos = jnp.arange(S, dtype=F32)
    inv_freq = ROPE_THETA ** (-jnp.arange(half, dtype=F32) * 2.0 / ROPE_DIM)
    ang = pos[:, None] * inv_freq[None, :]
    cos, sin = jnp.cos(ang), jnp.sin(ang)
    pad = HEAD64 - ROPE_DIM
    cos64 = jnp.concatenate([cos, cos, jnp.ones((S, pad), F32)], axis=-1)
    sin64 = jnp.concatenate([-sin, sin, jnp.zeros((S, pad), F32)], axis=-1)
    return jnp.tile(cos64, (1, 2)), jnp.tile(sin64, (1, 2))


def _lower_bounds(p):
    sm = jax.nn.softmax(p.astype(F32), axis=0)
    return jnp.cumsum(sm, axis=0) - sm[0:1]


def _tile4(w):
    return jnp.tile(w.astype(F32), (1, 4))


def _neg_stabiliser(wq, wk, width):
    bound = (HEAD64 ** 0.5 * LOG2E) * jnp.max(jnp.abs(wq), axis=-1) * jnp.max(jnp.abs(wk), axis=-1)
    bound = jnp.minimum(bound.astype(F32), EXP2_RANGE)
    return jnp.broadcast_to(-bound[:, None, None], (DEPTH, 1, width))


def _trunk(x, mem, prm):
    B, S, _ = x.shape
    assert S % ATT_TILE == 0 and ATT_TILE % PROJ_TILE == 0 and PROJ_TILE % SCAN_CHUNK == 0
    cos_t, sin_t = _rope_tables(S)
    for l in range(DEPTH):
        p = {k: v[l] for k, v in prm.items()}
        (o_sum, qtb, lst, dtile, q1, q4, q16, k1, k4, k16, v1, v4, v16, mq) = _front(
            x, p["norm_w"], p["w_in"], p["lbf"], p["lbb"], p["aqw"], p["akw"], p["mqw"],
            cos_t, sin_t)
        mk, mva, mvb = _mem_kv(mem, p["mem_norm_w"], p["mem_wkv"], p["mkw"])
        o_at = _attention(p["negm_at"], (q1, q4, q16), (k1, k4, k16), (v1, v4, v16))
        x = _out_stage(x, p["norm_w"], p["w_in"], o_sum, qtb, lst, dtile, o_at, mq, mk, mva, mvb,
                       p["negm_mem"], p["onw"], p["w_out"])
    return x


def kernel(x_prompt, x_sample, mem_prompt, mem_sample, norm_w, w_in, hgrn_lb_fwd, hgrn_lb_bwd,
           hgrn_onorm_w, attn_qnorm_w, attn_knorm_w, mem_norm_w, mem_wkv, mem_qnorm_w,
           mem_knorm_w, w_out):
    prm = {
        "norm_w": norm_w.astype(F32)[:, None, :],
        "w_in": w_in.astype(BF16),
        "lbf": _lower_bounds(hgrn_lb_fwd)[:, None, :],
        "lbb": _lower_bounds(hgrn_lb_bwd)[:, None, :],
        "onw": hgrn_onorm_w.astype(F32)[:, None, :],
        "aqw": _tile4(attn_qnorm_w)[:, None, :],
        "akw": _tile4(attn_knorm_w)[:, None, :],
        "mem_norm_w": mem_norm_w.astype(F32)[:, None, :],
        "mem_wkv": mem_wkv.astype(BF16),
        "mqw": _tile4(mem_qnorm_w)[:, None, :],
        "mkw": _tile4(mem_knorm_w)[:, None, :],
        "w_out": w_out.astype(BF16),
        "negm_at": _neg_stabiliser(attn_qnorm_w, attn_knorm_w, QBLK + 2 * BAND),
        "negm_mem": _neg_stabiliser(mem_qnorm_w, mem_knorm_w, MEM_TOKENS),
    }
    y_prompt = _trunk(x_prompt, mem_prompt, prm)
    y_sample = _trunk(x_sample, mem_sample, prm)
    return (y_prompt, y_sample)
```

```python
import functools

import jax
import jax.numpy as jnp
from jax import lax
from jax.experimental import pallas as pl
from jax.experimental.pallas import tpu as pltpu

F32 = jnp.float32
BF16 = jnp.bfloat16

D_MODEL = 1024
DEPTH = 4
HG_HEADS = 4
HG_D = 128
HG_WIDTH = HG_HEADS * HG_D
AT_WIDTH = 256
HEAD64 = 64
MEM_WIDTH = 256
MEM_TOKENS = 256
IN_WIDTH = 4096
DILATIONS = (1, 4, 16)
BAND = 64
ROPE_THETA = 500000.0
ROPE_DIM = 16
NORM_EPS = 1e-6
MASK_VALUE = -1e30
LANES = 128
SLAB_ROWS = 32

C_HQ, C_FF, C_FB, C_HI = 0, 512, 1024, 1536
C_AQ, C_AK, C_AV, C_MQ = 2048, 2304, 2560, 2816
C_GATE = 3072
FRONT_WIDTH = C_GATE

LOG2E = 1.4426950408889634
EXP2_RANGE = 63.0
PROJ_TILE = 512
ROW_BLK = 256
ATT_TILE = 2048
QBLK = 128
ATT_GROUP = 16
SCAN_CHUNK = 64
EXP2_CLAMP = 115.0
VMEM_LIMIT = 56 * 1024 * 1024


def _cparams(*sem):
    return pltpu.CompilerParams(dimension_semantics=sem, vmem_limit_bytes=VMEM_LIMIT)


def _sigmoid(z):
    return 1.0 / (1.0 + jnp.exp(-z))


def _lane_iota(n=LANES):
    return lax.broadcasted_iota(jnp.int32, (1, n), 1)


def _slabs(fn, *xs, rows=SLAB_ROWS, width=LANES):
    n_rows = max(x.shape[0] for x in xs)
    n_cols = xs[0].shape[1]
    row_blocks = []
    for r0 in range(0, n_rows, rows):
        col_blocks = []
        for c0 in range(0, n_cols, width):
            res = fn(*[x[(slice(None) if x.shape[0] == 1 else slice(r0, r0 + rows)), c0:c0 + width]
                       for x in xs])
            col_blocks.append(res if isinstance(res, tuple) else (res,))
        row_blocks.append([jnp.concatenate(parts, axis=1) if len(parts) > 1 else parts[0]
                           for parts in zip(*col_blocks)])
    outs = [jnp.concatenate(parts, axis=0) if len(parts) > 1 else parts[0]
            for parts in zip(*row_blocks)]
    return outs[0] if len(outs) == 1 else tuple(outs)


def _rms_norm_bf16(x, w):
    def fn(xs, ws):
        ms = jnp.mean(xs * xs, axis=-1, keepdims=True)
        return (xs * lax.rsqrt(ms + NORM_EPS) * ws).astype(BF16)
    return _slabs(fn, x, w, width=x.shape[1])


def _silu(z):
    return _slabs(lambda t: t * _sigmoid(t), z)


def _headnorm64(t, w):
    lo_mask = _lane_iota() < HEAD64
    parts = []
    for hp in range(t.shape[1] // LANES):
        th = t[:, hp * LANES:(hp + 1) * LANES]
        t2 = th * th
        lo = jnp.sum(jnp.where(lo_mask, t2, 0.0), axis=-1, keepdims=True)
        hi = jnp.sum(jnp.where(lo_mask, 0.0, t2), axis=-1, keepdims=True)
        ms = jnp.where(lo_mask, lo, hi) * (1.0 / HEAD64)
        parts.append(th * lax.rsqrt(ms + NORM_EPS))
    return jnp.concatenate(parts, axis=-1) * w


def _rope(t, cos_t, sin_t):
    lane = _lane_iota() % HEAD64
    parts = []
    for hp in range(t.shape[1] // LANES):
        th = t[:, hp * LANES:(hp + 1) * LANES]
        partner = jnp.where(lane < ROPE_DIM // 2,
                            pltpu.roll(th, LANES - ROPE_DIM // 2, 1),
                            pltpu.roll(th, ROPE_DIM // 2, 1))
        parts.append(th * cos_t + partner * sin_t)
    return jnp.concatenate(parts, axis=-1)


def _pair_attend(q2, k2, va, vb, bias):
    return _pairs_attend([(q2, k2, va, vb, bias)])[0]


def _pairs_attend(pairs):
    lo_mask = _lane_iota() < HEAD64
    nt = (((1,), (1,)), ((), ()))
    s = []
    for q2, k2, _, _, _ in pairs:
        zero = jnp.zeros_like(q2)
        s.append((lax.dot_general(jnp.where(lo_mask, q2, zero), k2, nt, preferred_element_type=F32),
                  lax.dot_general(jnp.where(lo_mask, zero, q2), k2, nt, preferred_element_type=F32)))
    prob = lambda sc, bias: jnp.exp2(sc + bias).astype(BF16)
    p = [(_slabs(prob, s_lo, pr[4], width=s_lo.shape[1]), _slabs(prob, s_hi, pr[4], width=s_hi.shape[1]))
         for (s_lo, s_hi), pr in zip(s, pairs)]
    o = [(jnp.dot(p_lo, pr[2], preferred_element_type=F32),
          jnp.dot(p_hi, pr[3], preferred_element_type=F32)) for (p_lo, p_hi), pr in zip(p, pairs)]
    return [(jnp.where(lo_mask, o_lo, o_hi), pltpu.roll(jnp.where(lo_mask, o_hi, o_lo), HEAD64, 1))
            for o_lo, o_hi in o]


def _front_kernel(x_ref, nw_ref, w_ref, lbf_ref, lbb_ref, aqw_ref, akw_ref, mqw_ref,
                  cos_ref, sin_ref,
                  osum_o, qtb_o, lst_o, dtile_o,
                  q1_o, q4_o, q16_o, k1_o, k4_o, k16_o, v1_o, v4_o, v16_o, mq_o,
                  st_ref, tmp_lo, tmp_hi, tmp4_lo, tmp4_hi):
    ts = x_ref.shape[0]
    C = SCAN_CHUNK
    chunks = range(ts // C)
    heads = range(HG_HEADS)
    nt_dims = (((1,), (1,)), ((), ()))
    tn_dims = (((0,), (0,)), ((), ()))
    row_sl = [slice(c * C, (c + 1) * C) for c in chunks]
    col_sl = [slice(hd * HG_D, (hd + 1) * HG_D) for hd in heads]

    @pl.when(pl.program_id(1) == 0)
    def _():
        st_ref[...] = jnp.zeros_like(st_ref)

    h = _rms_norm_bf16(x_ref[...], nw_ref[...])

    def proj(c0, n):
        return jnp.dot(h, w_ref[:, c0:c0 + n], preferred_element_type=F32)

    def gate_terms(z, lb_ref):
        def fn(zs, lb):
            sig = _sigmoid(zs)
            lf = jnp.log2(lb + (1.0 - lb) * sig)
            lf_hi = lf.astype(BF16)
            return lf_hi, (lf - lf_hi.astype(F32)).astype(BF16), (1.0 - lb) * (1.0 - sig)
        return _slabs(fn, z, lb_ref[...])

    def proj_rows(r0, c0, n):
        return jnp.dot(h[r0:r0 + ROW_BLK, :], w_ref[:, c0:c0 + n], preferred_element_type=F32)

    acc = {k: [] for k in ("q", "v", "ff", "fb", "aq", "ak")}
    for r0 in range(0, ts, ROW_BLK):
        z_q = proj_rows(r0, C_HQ, HG_WIDTH)
        z_i = proj_rows(r0, C_HI, HG_WIDTH)
        acc["q"].append(_silu(z_q))
        z_f = proj_rows(r0, C_FF, HG_WIDTH)
        acc["v"].append(z_i.astype(BF16))
        z_b = proj_rows(r0, C_FB, HG_WIDTH)
        acc["ff"].append(gate_terms(z_f, lbf_ref))
        acc["aq"].append(proj_rows(r0, C_AQ, AT_WIDTH))
        acc["ak"].append(proj_rows(r0, C_AK, AT_WIDTH))
        acc["fb"].append(gate_terms(z_b, lbb_ref))
    cat = lambda parts: jnp.concatenate(parts, axis=0)
    q, v_bf, z_aq, z_ak = cat(acc["q"]), cat(acc["v"]), cat(acc["aq"]), cat(acc["ak"])
    lfh_f, lfl_f, kk_f = (cat(parts) for parts in zip(*acc["ff"]))
    lfh_b, lfl_b, kk_b = (cat(parts) for parts in zip(*acc["fb"]))

    ri = lax.broadcasted_iota(jnp.int32, (C, C), 0)
    ci = lax.broadcasted_iota(jnp.int32, (C, C), 1)
    tri = [ci <= ri, ci >= ri]
    tri_bf = [jnp.tile(jnp.where(t, 1.0, 0.0).astype(BF16), (1, 2)) for t in tri]
    end_row = [C - 1, 0]
    mid_row = [C // 2 - 1, C // 2]
    lfh_d, lfl_d, kk_d = [lfh_f, lfh_b], [lfl_f, lfl_b], [kk_f, kk_b]
    dirs = range(2)

    b_all = [[None] * len(chunks) for _ in dirs]
    for d in dirs:
        for c in chunks:
            lf2 = jnp.concatenate([lfh_d[d][row_sl[c], :], lfl_d[d][row_sl[c], :]], axis=0)
            b_all[d][c] = jnp.dot(tri_bf[d], lf2, preferred_element_type=F32)
    z_av = proj(C_AV, AT_WIDTH)
    z_mq = proj(C_MQ, MEM_WIDTH)

    q_inter = [[None] * len(chunks) for _ in dirs]
    k_state = [[None] * len(chunks) for _ in dirs]
    q_intra = [[None] * len(chunks) for _ in dirs]
    k_intra = [[None] * len(chunks) for _ in dirs]
    dec = [[None] * len(chunks) for _ in dirs]
    off = jnp.zeros((1, HG_WIDTH), F32)
    tail = [None] * len(chunks)
    for c in reversed(chunks):
        tail[c] = jnp.exp2(off)
        off = off + b_all[1][c][end_row[1]:end_row[1] + 1, :]
    dtile_o[...] = jnp.exp2(off)

    def operands(bs, qs, ks, be, bm, *tl):
        qi = qs * jnp.exp2(bs)
        out = (qi.astype(BF16),
               (ks * jnp.exp2(be - bs)).astype(BF16),
               (qs * jnp.exp2(jnp.minimum(bs - bm, EXP2_CLAMP))).astype(BF16),
               (ks * jnp.exp2(jnp.minimum(bm - bs, EXP2_CLAMP))).astype(BF16))
        return out + tuple((qi * t).astype(BF16) for t in tl)

    for d in dirs:
        for c in chunks:
            b = b_all[d][c]
            b_end = b[end_row[d]:end_row[d] + 1, :]
            b_mid = b[mid_row[d]:mid_row[d] + 1, :]
            dec[d][c] = jnp.exp2(b_end)
            res = _slabs(operands, b, q[row_sl[c], :], kk_d[d][row_sl[c], :], b_end, b_mid,
                         *([tail[c]] if d == 1 else []))
            q_inter[d][c], k_state[d][c], q_intra[d][c], k_intra[d][c] = res[:4]
            if d == 1:
                qtb_o[row_sl[c], :] = res[4]

    zeros_k = jnp.zeros((C, HG_D), BF16)

    def both_scores(c, hd):
        lhs = jnp.concatenate([q_intra[d][c][:, col_sl[hd]] for d in dirs], axis=1)
        rhs = jnp.concatenate(
            [jnp.concatenate([k_intra[0][c][:, col_sl[hd]], zeros_k], axis=1),
             jnp.concatenate([zeros_k, k_intra[1][c][:, col_sl[hd]]], axis=1)], axis=0)
        return lax.dot_general(lhs, rhs, nt_dims, preferred_element_type=F32)

    sc = [[both_scores(c, hd) for hd in heads] for c in chunks]
    upd2 = [[lax.dot_general(v_bf[row_sl[c], col_sl[hd]],
                             jnp.concatenate([k_state[d][c][:, col_sl[hd]] for d in dirs], axis=1),
                             tn_dims, preferred_element_type=F32) for hd in heads] for c in chunks]
    upd = [[[upd2[c][hd][:, d * HG_D:(d + 1) * HG_D] for hd in heads] for c in chunks]
           for d in dirs]
    ri2 = lax.broadcasted_iota(jnp.int32, (C, 2 * C), 0)
    ci2 = lax.broadcasted_iota(jnp.int32, (C, 2 * C), 1)
    tri2 = (ci2 <= ri2) | (ci2 - C >= ri2)
    sc = [[jnp.where(tri2, sc[c][hd], 0.0).astype(BF16) for hd in heads] for c in chunks]

    st_in = [[[None] * HG_HEADS for _ in chunks] for _ in dirs]
    for hd in heads:
        def forward(st, *terms):
            seen = []
            for c in chunks:
                seen.append(st.astype(BF16))
                st = st * terms[2 * c] + terms[2 * c + 1]
            return (st, *seen)

        def backward(*terms):
            seen, st = [], None
            for c in reversed(chunks):
                if st is None:
                    st = terms[2 * c + 1]
                else:
                    seen.append(st.astype(BF16))
                    st = st * terms[2 * c] + terms[2 * c + 1]
            return (st, *seen)

        terms = [[t for c in chunks for t in (dec[d][c][:, col_sl[hd]], upd[d][c][hd])]
                 for d in dirs]
        st_new, *seen = _slabs(forward, st_ref[hd], *terms[0])
        st_ref[hd] = st_new
        for c in chunks:
            st_in[0][c][hd] = seen[c]
        st_new, *seen = _slabs(backward, *terms[1])
        lst_o[hd] = st_new
        for j, c in enumerate(reversed(chunks[:-1])):
            st_in[1][c][hd] = seen[j]
    zeros_st = jnp.zeros((HG_D, HG_D), BF16)
    o_parts = [[[] for _ in heads] for _ in chunks]
    for c in chunks:
        for hd in heads:
            v_c = v_bf[row_sl[c], col_sl[hd]]
            o_parts[c][hd].append(jnp.dot(sc[c][hd], jnp.concatenate([v_c, v_c], axis=0),
                                          preferred_element_type=F32))
            states = [zeros_st if st_in[d][c][hd] is None else st_in[d][c][hd] for d in dirs]
            o_parts[c][hd].append(lax.dot_general(
                jnp.concatenate([q_inter[d][c][:, col_sl[hd]] for d in dirs], axis=1),
                jnp.concatenate(states, axis=1), nt_dims, preferred_element_type=F32))

    def emit(val, o1, o4, o16):
        o1[...] = val.astype(BF16)
        for hp, (tmp, tmp4) in enumerate(((tmp_lo, tmp4_lo), (tmp_hi, tmp4_hi))):
            cols = slice(hp * LANES, (hp + 1) * LANES)
            tmp[...] = val[:, cols]
            for r4 in range(4):
                cls = tmp[pl.ds(r4, ts // 4, stride=4), :]
                o4[r4, :, cols] = cls.astype(BF16)
                tmp4[r4] = cls
                for j in range(4):
                    o16[r4 + 4 * j, :, cols] = tmp4[r4, pl.ds(j, ts // 16, stride=4), :].astype(BF16)

    cos_t = cos_ref[...]
    sin_t = sin_ref[...]
    scale = LOG2E * HEAD64 ** -0.5
    cos2, sin2 = jnp.tile(cos_t, (1, 2)), jnp.tile(sin_t, (1, 2))
    qk = lambda t, w, cs, sn: _rope(_headnorm64(t, w), cs[:, :LANES], sn[:, :LANES])
    emit(_slabs(lambda t, w, cs, sn: qk(t, w, cs, sn) * scale, z_aq, aqw_ref[...], cos2, sin2,
                width=AT_WIDTH), q1_o, q4_o, q16_o)
    emit(_slabs(qk, z_ak, akw_ref[...], cos2, sin2, width=AT_WIDTH), k1_o, k4_o, k16_o)
    emit(z_av, v1_o, v4_o, v16_o)
    mq_o[...] = _slabs(lambda t, w: (_headnorm64(t, w) * scale).astype(BF16), z_mq, mqw_ref[...],
                       width=MEM_WIDTH)
    for c in chunks:
        for hd in heads:
            o_sum = o_parts[c][hd][0]
            for part in o_parts[c][hd][1:]:
                o_sum = o_sum + part
            osum_o[row_sl[c], col_sl[hd]] = o_sum.astype(BF16)


def _front(x, nw, w_in, lbf, lbb, aqw, akw, mqw, cos_t, sin_t):
    B, S, _ = x.shape
    ts = PROJ_TILE
    nt = S // ts
    per_att = ATT_TILE // ts
    row = lambda n: pl.BlockSpec((None, ts, n), lambda b, i: (b, i, 0))
    vec = lambda n: pl.BlockSpec((1, n), lambda b, i: (0, 0))
    tab = pl.BlockSpec((ts, LANES), lambda b, i: (i, 0))

    def grouped(dil):
        spec = pl.BlockSpec((None, None, dil, ts // dil, AT_WIDTH),
                            lambda b, i: (b, i // per_att, 0, i % per_att, 0))
        shape = jax.ShapeDtypeStruct((B, S // ATT_TILE, dil, ATT_TILE // dil, AT_WIDTH), BF16)
        return spec, shape

    outs = [(row(HG_WIDTH), jax.ShapeDtypeStruct((B, S, HG_WIDTH), BF16)),
            (row(HG_WIDTH), jax.ShapeDtypeStruct((B, S, HG_WIDTH), BF16)),
            (pl.BlockSpec((None, None, HG_HEADS, HG_D, HG_D), lambda b, i: (b, i, 0, 0, 0)),
             jax.ShapeDtypeStruct((B, nt, HG_HEADS, HG_D, HG_D), F32)),
            (pl.BlockSpec((None, None, 1, HG_WIDTH), lambda b, i: (b, i, 0, 0)),
             jax.ShapeDtypeStruct((B, nt, 1, HG_WIDTH), F32))]
    for _ in range(3):
        outs.append((row(AT_WIDTH), jax.ShapeDtypeStruct((B, S, AT_WIDTH), BF16)))
        outs.append(grouped(4))
        outs.append(grouped(16))
    outs.append((row(MEM_WIDTH), jax.ShapeDtypeStruct((B, S, MEM_WIDTH), BF16)))
    return pl.pallas_call(
        _front_kernel,
        grid=(B, nt),
        in_specs=[row(D_MODEL), vec(D_MODEL),
                  pl.BlockSpec((D_MODEL, FRONT_WIDTH), lambda b, i: (0, 0)),
                  vec(HG_WIDTH), vec(HG_WIDTH), vec(AT_WIDTH), vec(AT_WIDTH), vec(MEM_WIDTH),
                  tab, tab],
        out_specs=[s for s, _ in outs],
        out_shape=[s for _, s in outs],
        scratch_shapes=[pltpu.VMEM((HG_HEADS, HG_D, HG_D), F32)]
                       + [pltpu.VMEM((ts, LANES), F32)] * 2
                       + [pltpu.VMEM((4, ts // 4, LANES), F32)] * 2,
        compiler_params=_cparams("parallel", "arbitrary"),
        name="front",
    )(x, nw, w_in, lbf, lbb, aqw, akw, mqw, cos_t, sin_t)


def _mem_kv_kernel(m_ref, nw_ref, w_ref, kw_ref, mk_o, mva_o, mvb_o):
    h = _rms_norm_bf16(m_ref[...], nw_ref[...])
    kv = jnp.dot(h, w_ref[...], preferred_element_type=F32)
    mk_o[...] = _headnorm64(kv[:, :MEM_WIDTH], kw_ref[...]).astype(BF16)
    lo_mask = (lax.broadcasted_iota(jnp.int32, (1, MEM_WIDTH), 1) % LANES) < HEAD64
    mv = kv[:, MEM_WIDTH:]
    mva_o[...] = jnp.where(lo_mask, mv, 1.0).astype(BF16)
    mvb_o[...] = jnp.where(lo_mask, 1.0, mv).astype(BF16)


def _mem_kv(mem, nw, wkv, kw):
    B, M, _ = mem.shape
    out = pl.BlockSpec((None, M, MEM_WIDTH), lambda b: (b, 0, 0))
    return pl.pallas_call(
        _mem_kv_kernel,
        grid=(B,),
        in_specs=[pl.BlockSpec((None, M, D_MODEL), lambda b: (b, 0, 0)),
                  pl.BlockSpec((1, D_MODEL), lambda b: (0, 0)),
                  pl.BlockSpec((D_MODEL, 2 * MEM_WIDTH), lambda b: (0, 0)),
                  pl.BlockSpec((1, MEM_WIDTH), lambda b: (0, 0))],
        out_specs=[out, out, out],
        out_shape=[jax.ShapeDtypeStruct((B, M, MEM_WIDTH), BF16)] * 3,
        compiler_params=_cparams("parallel"),
        name="mem_kv",
    )(mem, nw, wkv, kw)


def _attn_kernel(negm_ref, q1_ref, q4_ref, q16_ref,
                 k1p, k1m, k1n, k4p, k4m, k4n, k16p, k16m, k16n,
                 v1p, v1m, v1n, v4p, v4m, v4n, v16p, v16m, v16n,
                 o_ref, num_lo, num_hi, den_lo, den_hi, bias_s, *, nt):
    num_s = (num_lo, num_hi)
    den_s = (den_lo, den_hi)
    i = pl.program_id(1)
    lo_mask = _lane_iota() < HEAD64
    WIN = QBLK + 2 * BAND

    qi = lax.broadcasted_iota(jnp.int32, (QBLK, WIN), 0)
    ke = lax.broadcasted_iota(jnp.int32, (QBLK, WIN), 1)
    delta = ke - BAND - qi
    band = (delta <= BAND) & (delta >= -BAND)
    left_ok = (ke >= BAND) | (i > 0)
    right_ok = (ke < QBLK + BAND) | (i < nt - 1)
    for kind, ok in enumerate((band, band & left_ok, band & right_ok, band & left_ok & right_ok)):
        bias_s[kind] = jnp.where(ok, negm_ref[...], MASK_VALUE)

    def pattern(dil, q_ref, kp, km, kn, vp, vm, vn, assign):
        rows = ATT_TILE // dil
        blocks_per_res = rows // QBLK
        n_groups = dil * blocks_per_res // ATT_GROUP

        def window(p_ref, m_ref, n_ref, res, blk):
            a = blk * QBLK
            parts = [p_ref[res]] if blk == 0 else [m_ref[res, a - BAND:a, :]]
            parts.append(m_ref[res, a:a + QBLK, :])
            parts.append(n_ref[res] if blk == blocks_per_res - 1
                         else m_ref[res, a + QBLK:a + QBLK + BAND, :])
            return jnp.concatenate(parts, axis=0)

        def group(grp, res_of, blk_of):
            pairs, dests = [], []
            for u in range(ATT_GROUP):
                res, blk = res_of(grp, u), blk_of(u)
                a = blk * QBLK
                kind = (1 if blk == 0 else 0) + (2 if blk == blocks_per_res - 1 else 0)
                k_win = window(kp, km, kn, res, blk)
                v_win = window(vp, vm, vn, res, blk)
                out_rows = pl.ds(a, QBLK) if dil == 1 else pl.ds(a * dil + res, QBLK, stride=dil)
                for hp in range(AT_WIDTH // LANES):
                    cols = slice(hp * LANES, (hp + 1) * LANES)
                    v2 = v_win[:, cols]
                    one = jnp.ones_like(v2)
                    pairs.append((q_ref[res, a:a + QBLK, cols], k_win[:, cols],
                                  jnp.where(lo_mask, v2, one), jnp.where(lo_mask, one, v2),
                                  bias_s[kind]))
                    dests.append((hp, out_rows))
            for (hp, out_rows), (num, den) in zip(dests, _pairs_attend(pairs)):
                if assign:
                    num_s[hp][out_rows, :] = num
                    den_s[hp][out_rows, :] = den
                else:
                    num_s[hp][out_rows, :] += num
                    den_s[hp][out_rows, :] += den

        if blocks_per_res >= ATT_GROUP:
            per_res = blocks_per_res // ATT_GROUP
            for res in range(dil):
                for g in range(per_res):
                    group(0, lambda grp, u, res=res: res, lambda u, g=g: g * ATT_GROUP + u)
        else:
            res_per_group = ATT_GROUP // blocks_per_res

            def body(grp, carry):
                group(grp, lambda grp, u: grp * res_per_group + u // blocks_per_res,
                      lambda u: u % blocks_per_res)
                return carry

            lax.fori_loop(0, n_groups, body, 0)

    pattern(16, q16_ref, k16p, k16m, k16n, v16p, v16m, v16n, True)
    pattern(4, q4_ref, k4p, k4m, k4n, v4p, v4m, v4n, False)
    pattern(1, q1_ref, k1p, k1m, k1n, v1p, v1m, v1n, False)
    for hp in range(AT_WIDTH // LANES):
        o_ref[:, hp * LANES:(hp + 1) * LANES] = (num_s[hp][...] / den_s[hp][...]).astype(BF16)


def _attention(negm, q, k, v):
    B, S, _ = q[0].shape
    nt = S // ATT_TILE

    def view(t, dil):
        return t.reshape(B, nt, dil, ATT_TILE // dil, AT_WIDTH)

    def main(dil):
        return pl.BlockSpec((None, None, dil, ATT_TILE // dil, AT_WIDTH),
                            lambda b, i: (b, i, 0, 0, 0))

    def prev(dil):
        lastblk = ATT_TILE // dil // BAND - 1
        return pl.BlockSpec((None, None, dil, BAND, AT_WIDTH),
                            lambda b, i: (b, jnp.maximum(i - 1, 0), 0, lastblk, 0))

    def nxt(dil):
        return pl.BlockSpec((None, None, dil, BAND, AT_WIDTH),
                            lambda b, i: (b, jnp.minimum(i + 1, nt - 1), 0, 0, 0))

    qs = [view(t, d) for t, d in zip(q, DILATIONS)]
    ks = [view(t, d) for t, d in zip(k, DILATIONS)]
    vs = [view(t, d) for t, d in zip(v, DILATIONS)]
    halo_specs, halo_args = [], []
    for ts_ in (ks, vs):
        for t, d in zip(ts_, DILATIONS):
            halo_specs += [prev(d), main(d), nxt(d)]
            halo_args += [t, t, t]
    return pl.pallas_call(
        functools.partial(_attn_kernel, nt=nt),
        grid=(B, nt),
        in_specs=[pl.BlockSpec((1, QBLK + 2 * BAND), lambda b, i: (0, 0))]
                 + [main(d) for d in DILATIONS] + halo_specs,
        out_specs=pl.BlockSpec((None, ATT_TILE, AT_WIDTH), lambda b, i: (b, i, 0)),
        out_shape=jax.ShapeDtypeStruct((B, S, AT_WIDTH), BF16),
        scratch_shapes=[pltpu.VMEM((ATT_TILE, LANES), F32)] * 4
                       + [pltpu.VMEM((4, QBLK, QBLK + 2 * BAND), F32)],
        compiler_params=_cparams("parallel", "parallel"),
        name="dilated_attn",
    )(negm, *qs, *halo_args)


def _out_kernel(x_ref, nw_ref, wg_ref, osum_ref, qtb_ref, lst_ref, dtile_ref, oat_ref, mq_ref,
                mk_ref, mva_ref, mvb_ref, negm_ref, onw_ref, w_ref, y_ref, sin_ref):
    nt_dims = (((1,), (1,)), ((), ()))

    @pl.when(pl.program_id(1) == 0)
    def _():
        sin_ref[...] = jnp.zeros_like(sin_ref)

    h = _rms_norm_bf16(x_ref[...], nw_ref[...])
    z_g = jnp.dot(h, wg_ref[...], preferred_element_type=F32)
    mem = [_pair_attend(mq_ref[:, cols], mk_ref[:, cols], mva_ref[:, cols], mvb_ref[:, cols],
                        negm_ref[...])
           for cols in (slice(0, LANES), slice(LANES, 2 * LANES))]
    corr = [lax.dot_general(qtb_ref[:, hd * HG_D:(hd + 1) * HG_D], sin_ref[hd].astype(BF16),
                            nt_dims, preferred_element_type=F32) for hd in range(HG_HEADS)]
    dtile = dtile_ref[...]
    parts = []

    def hgrn_out(o_sum, cr, zg, w):
        t = o_sum.astype(F32) + cr
        ms = jnp.mean(t * t, axis=-1, keepdims=True)
        return (t * lax.rsqrt(ms + NORM_EPS) * w * (zg * _sigmoid(zg))).astype(BF16)

    for hd in range(HG_HEADS):
        cols = slice(hd * HG_D, (hd + 1) * HG_D)
        sin_ref[hd] = sin_ref[hd] * dtile[:, cols] + lst_ref[hd]
        parts.append(_slabs(hgrn_out, osum_ref[:, cols], corr[hd], z_g[:, cols], onw_ref[...]))
    c0 = HG_WIDTH
    parts.append(_slabs(lambda o, zg: (o.astype(F32) * (zg * _sigmoid(zg))).astype(BF16),
                        oat_ref[...], z_g[:, c0:c0 + AT_WIDTH]))
    c0 += AT_WIDTH
    for hp, (num, den) in enumerate(mem):
        parts.append(_slabs(lambda n, d, zg: (n / d * (zg * _sigmoid(zg))).astype(BF16),
                            num, den, z_g[:, c0 + hp * LANES:c0 + (hp + 1) * LANES]))
    mixed = jnp.concatenate(parts, axis=-1)
    y_ref[...] = x_ref[...] + jnp.dot(mixed, w_ref[...], preferred_element_type=F32)


def _out_stage(x, nw, w_in, o_sum, qtb, lst, dtile, o_at, mq, mk, mva, mvb, negm, onw, w_out):
    B, S, _ = x.shape
    ts = PROJ_TILE
    nt = S // ts
    row = lambda n: pl.BlockSpec((None, ts, n), lambda b, i: (b, nt - 1 - i, 0))
    memspec = pl.BlockSpec((None, MEM_TOKENS, MEM_WIDTH), lambda b, i: (b, 0, 0))
    return pl.pallas_call(
        _out_kernel,
        grid=(B, nt),
        in_specs=[row(D_MODEL), pl.BlockSpec((1, D_MODEL), lambda b, i: (0, 0)),
                  pl.BlockSpec((D_MODEL, IN_WIDTH - C_GATE),
                               lambda b, i: (0, C_GATE // (IN_WIDTH - C_GATE))),
                  row(HG_WIDTH), row(HG_WIDTH),
                  pl.BlockSpec((None, None, HG_HEADS, HG_D, HG_D),
                               lambda b, i: (b, nt - 1 - i, 0, 0, 0)),
                  pl.BlockSpec((None, None, 1, HG_WIDTH), lambda b, i: (b, nt - 1 - i, 0, 0)),
                  row(AT_WIDTH), row(MEM_WIDTH), memspec, memspec, memspec,
                  pl.BlockSpec((1, MEM_TOKENS), lambda b, i: (0, 0)),
                  pl.BlockSpec((1, HG_D), lambda b, i: (0, 0)),
                  pl.BlockSpec((D_MODEL, D_MODEL), lambda b, i: (0, 0))],
        out_specs=row(D_MODEL),
        out_shape=jax.ShapeDtypeStruct((B, S, D_MODEL), F32),
        scratch_shapes=[pltpu.VMEM((HG_HEADS, HG_D, HG_D), F32)],
        compiler_params=_cparams("parallel", "arbitrary"),
        name="out_stage",
    )(x, nw, w_in, o_sum, qtb, lst, dtile, o_at, mq, mk, mva, mvb, negm, onw, w_out)


def _rope_tables(S):
    half = ROPE_DIM // 2
    pos = jnp.arange(S, dtype=F32)
    inv_freq = ROPE_THETA ** (-jnp.arange(half, dtype=F32) * 2.0 / ROPE_DIM)
    ang = pos[:, None] * inv_freq[None, :]
    cos, sin = jnp.cos(ang), jnp.sin(ang)
    pad = HEAD64 - ROPE_DIM
    cos64 = jnp.concatenate([cos, cos, jnp.ones((S, pad), F32)], axis=-1)
    sin64 = jnp.concatenate([-sin, sin, jnp.zeros((S, pad), F32)], axis=-1)
    return jnp.tile(cos64, (1, 2)), jnp.tile(sin64, (1, 2))


def _lower_bounds(p):
    sm = jax.nn.softmax(p.astype(F32), axis=0)
    return jnp.cumsum(sm, axis=0) - sm[0:1]


def _tile4(w):
    return jnp.tile(w.astype(F32), (1, 4))


def _neg_stabiliser(wq, wk, width):
    bound = (HEAD64 ** 0.5 * LOG2E) * jnp.max(jnp.abs(wq), axis=-1) * jnp.max(jnp.abs(wk), axis=-1)
    bound = jnp.minimum(bound.astype(F32), EXP2_RANGE)
    return jnp.broadcast_to(-bound[:, None, None], (DEPTH, 1, width))


def _trunk(x, mem, prm):
    B, S, _ = x.shape
    assert S % ATT_TILE == 0 and ATT_TILE % PROJ_TILE == 0 and PROJ_TILE % SCAN_CHUNK == 0
    cos_t, sin_t = _rope_tables(S)
    for l in range(DEPTH):
        p = {k: v[l] for k, v in prm.items()}
        (o_sum, qtb, lst, dtile, q1, q4, q16, k1, k4, k16, v1, v4, v16, mq) = _front(
            x, p["norm_w"], p["w_in"], p["lbf"], p["lbb"], p["aqw"], p["akw"], p["mqw"],
            cos_t, sin_t)
        mk, mva, mvb = _mem_kv(mem, p["mem_norm_w"], p["mem_wkv"], p["mkw"])
        o_at = _attention(p["negm_at"], (q1, q4, q16), (k1, k4, k16), (v1, v4, v16))
        x = _out_stage(x, p["norm_w"], p["w_in"], o_sum, qtb, lst, dtile, o_at, mq, mk, mva, mvb,
                       p["negm_mem"], p["onw"], p["w_out"])
    return x


def kernel(x_prompt, x_sample, mem_prompt, mem_sample, norm_w, w_in, hgrn_lb_fwd, hgrn_lb_bwd,
           hgrn_onorm_w, attn_qnorm_w, attn_knorm_w, mem_norm_w, mem_wkv, mem_qnorm_w,
           mem_knorm_w, w_out):
    prm = {
        "norm_w": norm_w.astype(F32)[:, None, :],
        "w_in": w_in.astype(BF16),
        "lbf": _lower_bounds(hgrn_lb_fwd)[:, None, :],
        "lbb": _lower_bounds(hgrn_lb_bwd)[:, None, :],
        "onw": hgrn_onorm_w.astype(F32)[:, None, :],
        "aqw": _tile4(attn_qnorm_w)[:, None, :],
        "akw": _tile4(attn_knorm_w)[:, None, :],
        "mem_norm_w": mem_norm_w.astype(F32)[:, None, :],
        "mem_wkv": mem_wkv.astype(BF16),
        "mqw": _tile4(mem_qnorm_w)[:, None, :],
        "mkw": _tile4(mem_knorm_w)[:, None, :],
        "w_out": w_out.astype(BF16),
        "negm_at": _neg_stabiliser(attn_qnorm_w, attn_knorm_w, QBLK + 2 * BAND),
        "negm_mem": _neg_stabiliser(mem_qnorm_w, mem_knorm_w, MEM_TOKENS),
    }
    y_prompt = _trunk(x_prompt, mem_prompt, prm)
    y_sample = _trunk(x_sample, mem_sample, prm)
    return (y_prompt, y_sample)
```

```python
import functools

import jax
import jax.numpy as jnp
from jax import lax
from jax.experimental import pallas as pl
from jax.experimental.pallas import tpu as pltpu

F32 = jnp.float32
BF16 = jnp.bfloat16

D_MODEL = 1024
DEPTH = 4
HG_HEADS = 4
HG_D = 128
HG_WIDTH = HG_HEADS * HG_D
AT_WIDTH = 256
HEAD64 = 64
MEM_WIDTH = 256
MEM_TOKENS = 256
IN_WIDTH = 4096
DILATIONS = (1, 4, 16)
BAND = 64
ROPE_THETA = 500000.0
ROPE_DIM = 16
NORM_EPS = 1e-6
MASK_VALUE = -1e30
LANES = 128
SLAB_ROWS = 32

C_HQ, C_FF, C_FB, C_HI = 0, 512, 1024, 1536
C_AQ, C_AK, C_AV, C_MQ = 2048, 2304, 2560, 2816
C_GATE = 3072
FRONT_WIDTH = C_GATE

LOG2E = 1.4426950408889634
EXP2_RANGE = 63.0
PROJ_TILE = 512
OUT_TILE = 1024
ROW_BLK = 256
ATT_TILE = 2048
QBLK = 128
ATT_GROUP = 16
SCAN_CHUNK = 64
EXP2_CLAMP = 115.0
VMEM_LIMIT = 56 * 1024 * 1024


def _cparams(*sem):
    return pltpu.CompilerParams(dimension_semantics=sem, vmem_limit_bytes=VMEM_LIMIT)


def _sigmoid(z):
    return 1.0 / (1.0 + jnp.exp(-z))


def _lane_iota(n=LANES):
    return lax.broadcasted_iota(jnp.int32, (1, n), 1)


def _slabs(fn, *xs, rows=SLAB_ROWS, width=LANES):
    n_rows = max(x.shape[0] for x in xs)
    n_cols = xs[0].shape[1]
    row_blocks = []
    for r0 in range(0, n_rows, rows):
        col_blocks = []
        for c0 in range(0, n_cols, width):
            res = fn(*[x[(slice(None) if x.shape[0] == 1 else slice(r0, r0 + rows)), c0:c0 + width]
                       for x in xs])
            col_blocks.append(res if isinstance(res, tuple) else (res,))
        row_blocks.append([jnp.concatenate(parts, axis=1) if len(parts) > 1 else parts[0]
                           for parts in zip(*col_blocks)])
    outs = [jnp.concatenate(parts, axis=0) if len(parts) > 1 else parts[0]
            for parts in zip(*row_blocks)]
    return outs[0] if len(outs) == 1 else tuple(outs)


def _rms_norm_bf16(x, w):
    def fn(xs, ws):
        ms = jnp.mean(xs * xs, axis=-1, keepdims=True)
        return (xs * lax.rsqrt(ms + NORM_EPS) * ws).astype(BF16)
    return _slabs(fn, x, w, width=x.shape[1])


def _silu(z):
    return _slabs(lambda t: t * _sigmoid(t), z)


def _headnorm64(t, w):
    lo_mask = _lane_iota() < HEAD64
    parts = []
    for hp in range(t.shape[1] // LANES):
        th = t[:, hp * LANES:(hp + 1) * LANES]
        t2 = th * th
        lo = jnp.sum(jnp.where(lo_mask, t2, 0.0), axis=-1, keepdims=True)
        hi = jnp.sum(jnp.where(lo_mask, 0.0, t2), axis=-1, keepdims=True)
        ms = jnp.where(lo_mask, lo, hi) * (1.0 / HEAD64)
        parts.append(th * lax.rsqrt(ms + NORM_EPS))
    return jnp.concatenate(parts, axis=-1) * w


def _rope(t, cos_t, sin_t):
    lane = _lane_iota() % HEAD64
    parts = []
    for hp in range(t.shape[1] // LANES):
        th = t[:, hp * LANES:(hp + 1) * LANES]
        partner = jnp.where(lane < ROPE_DIM // 2,
                            pltpu.roll(th, LANES - ROPE_DIM // 2, 1),
                            pltpu.roll(th, ROPE_DIM // 2, 1))
        parts.append(th * cos_t + partner * sin_t)
    return jnp.concatenate(parts, axis=-1)


def _pair_attend(q2, k2, va, vb, bias):
    return _pairs_attend([(q2, k2, va, vb, bias)])[0]


def _pairs_attend(pairs):
    lo_mask = _lane_iota() < HEAD64
    nt = (((1,), (1,)), ((), ()))
    s = []
    for q2, k2, _, _, _ in pairs:
        zero = jnp.zeros_like(q2)
        s.append((lax.dot_general(jnp.where(lo_mask, q2, zero), k2, nt, preferred_element_type=F32),
                  lax.dot_general(jnp.where(lo_mask, zero, q2), k2, nt, preferred_element_type=F32)))
    prob = lambda sc, bias: jnp.exp2(sc + bias).astype(BF16)
    p = [(_slabs(prob, s_lo, pr[4], width=s_lo.shape[1]), _slabs(prob, s_hi, pr[4], width=s_hi.shape[1]))
         for (s_lo, s_hi), pr in zip(s, pairs)]
    o = [(jnp.dot(p_lo, pr[2], preferred_element_type=F32),
          jnp.dot(p_hi, pr[3], preferred_element_type=F32)) for (p_lo, p_hi), pr in zip(p, pairs)]
    return [(jnp.where(lo_mask, o_lo, o_hi), pltpu.roll(jnp.where(lo_mask, o_hi, o_lo), HEAD64, 1))
            for o_lo, o_hi in o]


def _front_kernel(x_ref, nw_ref, w_ref, lbf_ref, lbb_ref, aqw_ref, akw_ref, mqw_ref,
                  cos_ref, sin_ref,
                  osum_o, qtb_o, lst_o, dtile_o,
                  q1_o, q4_o, q16_o, k1_o, k4_o, k16_o, v1_o, v4_o, v16_o, mq_o,
                  st_ref, tmp_lo, tmp_hi, tmp4_lo, tmp4_hi):
    ts = x_ref.shape[0]
    C = SCAN_CHUNK
    chunks = range(ts // C)
    heads = range(HG_HEADS)
    nt_dims = (((1,), (1,)), ((), ()))
    tn_dims = (((0,), (0,)), ((), ()))
    row_sl = [slice(c * C, (c + 1) * C) for c in chunks]
    col_sl = [slice(hd * HG_D, (hd + 1) * HG_D) for hd in heads]

    @pl.when(pl.program_id(1) == 0)
    def _():
        st_ref[...] = jnp.zeros_like(st_ref)

    h = _rms_norm_bf16(x_ref[...], nw_ref[...])

    def proj(c0, n):
        return jnp.dot(h, w_ref[:, c0:c0 + n], preferred_element_type=F32)

    def gate_terms(z, lb_ref):
        def fn(zs, lb):
            sig = _sigmoid(zs)
            lf = jnp.log2(lb + (1.0 - lb) * sig)
            lf_hi = lf.astype(BF16)
            return lf_hi, (lf - lf_hi.astype(F32)).astype(BF16), (1.0 - lb) * (1.0 - sig)
        return _slabs(fn, z, lb_ref[...])

    def proj_rows(r0, c0, n):
        return jnp.dot(h[r0:r0 + ROW_BLK, :], w_ref[:, c0:c0 + n], preferred_element_type=F32)

    acc = {k: [] for k in ("q", "v", "ff", "fb", "aq", "ak")}
    for r0 in range(0, ts, ROW_BLK):
        z_q = proj_rows(r0, C_HQ, HG_WIDTH)
        z_i = proj_rows(r0, C_HI, HG_WIDTH)
        acc["q"].append(_silu(z_q))
        z_f = proj_rows(r0, C_FF, HG_WIDTH)
        acc["v"].append(z_i.astype(BF16))
        z_b = proj_rows(r0, C_FB, HG_WIDTH)
        acc["ff"].append(gate_terms(z_f, lbf_ref))
        acc["aq"].append(proj_rows(r0, C_AQ, AT_WIDTH))
        acc["ak"].append(proj_rows(r0, C_AK, AT_WIDTH))
        acc["fb"].append(gate_terms(z_b, lbb_ref))
    cat = lambda parts: jnp.concatenate(parts, axis=0)
    q, v_bf, z_aq, z_ak = cat(acc["q"]), cat(acc["v"]), cat(acc["aq"]), cat(acc["ak"])
    lfh_f, lfl_f, kk_f = (cat(parts) for parts in zip(*acc["ff"]))
    lfh_b, lfl_b, kk_b = (cat(parts) for parts in zip(*acc["fb"]))

    ri = lax.broadcasted_iota(jnp.int32, (C, C), 0)
    ci = lax.broadcasted_iota(jnp.int32, (C, C), 1)
    tri = [ci <= ri, ci >= ri]
    tri_bf = [jnp.tile(jnp.where(t, 1.0, 0.0).astype(BF16), (1, 2)) for t in tri]
    end_row = [C - 1, 0]
    mid_row = [C // 2 - 1, C // 2]
    lfh_d, lfl_d, kk_d = [lfh_f, lfh_b], [lfl_f, lfl_b], [kk_f, kk_b]
    dirs = range(2)

    b_all = [[None] * len(chunks) for _ in dirs]
    for d in dirs:
        for c in chunks:
            lf2 = jnp.concatenate([lfh_d[d][row_sl[c], :], lfl_d[d][row_sl[c], :]], axis=0)
            b_all[d][c] = jnp.dot(tri_bf[d], lf2, preferred_element_type=F32)
    z_av = proj(C_AV, AT_WIDTH)
    z_mq = proj(C_MQ, MEM_WIDTH)

    q_inter = [[None] * len(chunks) for _ in dirs]
    k_state = [[None] * len(chunks) for _ in dirs]
    q_intra = [[None] * len(chunks) for _ in dirs]
    k_intra = [[None] * len(chunks) for _ in dirs]
    dec = [[None] * len(chunks) for _ in dirs]
    off = jnp.zeros((1, HG_WIDTH), F32)
    tail = [None] * len(chunks)
    for c in reversed(chunks):
        tail[c] = jnp.exp2(off)
        off = off + b_all[1][c][end_row[1]:end_row[1] + 1, :]
    dtile_o[...] = jnp.exp2(off)

    def operands(bs, qs, ks, be, bm, *tl):
        qi = qs * jnp.exp2(bs)
        out = (qi.astype(BF16),
               (ks * jnp.exp2(be - bs)).astype(BF16),
               (qs * jnp.exp2(jnp.minimum(bs - bm, EXP2_CLAMP))).astype(BF16),
               (ks * jnp.exp2(jnp.minimum(bm - bs, EXP2_CLAMP))).astype(BF16))
        return out + tuple((qi * t).astype(BF16) for t in tl)

    for d in dirs:
        for c in chunks:
            b = b_all[d][c]
            b_end = b[end_row[d]:end_row[d] + 1, :]
            b_mid = b[mid_row[d]:mid_row[d] + 1, :]
            dec[d][c] = jnp.exp2(b_end)
            res = _slabs(operands, b, q[row_sl[c], :], kk_d[d][row_sl[c], :], b_end, b_mid,
                         *([tail[c]] if d == 1 else []))
            q_inter[d][c], k_state[d][c], q_intra[d][c], k_intra[d][c] = res[:4]
            if d == 1:
                qtb_o[row_sl[c], :] = res[4]

    zeros_k = jnp.zeros((C, HG_D), BF16)

    def both_scores(c, hd):
        lhs = jnp.concatenate([q_intra[d][c][:, col_sl[hd]] for d in dirs], axis=1)
        rhs = jnp.concatenate(
            [jnp.concatenate([k_intra[0][c][:, col_sl[hd]], zeros_k], axis=1),
             jnp.concatenate([zeros_k, k_intra[1][c][:, col_sl[hd]]], axis=1)], axis=0)
        return lax.dot_general(lhs, rhs, nt_dims, preferred_element_type=F32)

    sc = [[both_scores(c, hd) for hd in heads] for c in chunks]
    upd2 = [[lax.dot_general(v_bf[row_sl[c], col_sl[hd]],
                             jnp.concatenate([k_state[d][c][:, col_sl[hd]] for d in dirs], axis=1),
                             tn_dims, preferred_element_type=F32) for hd in heads] for c in chunks]
    upd = [[[upd2[c][hd][:, d * HG_D:(d + 1) * HG_D] for hd in heads] for c in chunks]
           for d in dirs]
    ri2 = lax.broadcasted_iota(jnp.int32, (C, 2 * C), 0)
    ci2 = lax.broadcasted_iota(jnp.int32, (C, 2 * C), 1)
    tri2 = (ci2 <= ri2) | (ci2 - C >= ri2)
    sc = [[jnp.where(tri2, sc[c][hd], 0.0).astype(BF16) for hd in heads] for c in chunks]

    st_in = [[[None] * HG_HEADS for _ in chunks] for _ in dirs]
    for hd in heads:
        def forward(st, *terms):
            seen = []
            for c in chunks:
                seen.append(st.astype(BF16))
                st = st * terms[2 * c] + terms[2 * c + 1]
            return (st, *seen)

        def backward(*terms):
            seen, st = [], None
            for c in reversed(chunks):
                if st is None:
                    st = terms[2 * c + 1]
                else:
                    seen.append(st.astype(BF16))
                    st = st * terms[2 * c] + terms[2 * c + 1]
            return (st, *seen)

        terms = [[t for c in chunks for t in (dec[d][c][:, col_sl[hd]], upd[d][c][hd])]
                 for d in dirs]
        st_new, *seen = _slabs(forward, st_ref[hd], *terms[0])
        st_ref[hd] = st_new
        for c in chunks:
            st_in[0][c][hd] = seen[c]
        st_new, *seen = _slabs(backward, *terms[1])
        lst_o[hd] = st_new
        for j, c in enumerate(reversed(chunks[:-1])):
            st_in[1][c][hd] = seen[j]
    zeros_st = jnp.zeros((HG_D, HG_D), BF16)
    o_parts = [[[] for _ in heads] for _ in chunks]
    for c in chunks:
        for hd in heads:
            v_c = v_bf[row_sl[c], col_sl[hd]]
            o_parts[c][hd].append(jnp.dot(sc[c][hd], jnp.concatenate([v_c, v_c], axis=0),
                                          preferred_element_type=F32))
            states = [zeros_st if st_in[d][c][hd] is None else st_in[d][c][hd] for d in dirs]
            o_parts[c][hd].append(lax.dot_general(
                jnp.concatenate([q_inter[d][c][:, col_sl[hd]] for d in dirs], axis=1),
                jnp.concatenate(states, axis=1), nt_dims, preferred_element_type=F32))

    def emit(val, o1, o4, o16):
        o1[...] = val.astype(BF16)
        for hp, (tmp, tmp4) in enumerate(((tmp_lo, tmp4_lo), (tmp_hi, tmp4_hi))):
            cols = slice(hp * LANES, (hp + 1) * LANES)
            tmp[...] = val[:, cols]
            for r4 in range(4):
                cls = tmp[pl.ds(r4, ts // 4, stride=4), :]
                o4[r4, :, cols] = cls.astype(BF16)
                tmp4[r4] = cls
                for j in range(4):
                    o16[r4 + 4 * j, :, cols] = tmp4[r4, pl.ds(j, ts // 16, stride=4), :].astype(BF16)

    cos_t = cos_ref[...]
    sin_t = sin_ref[...]
    scale = LOG2E * HEAD64 ** -0.5
    cos2, sin2 = jnp.tile(cos_t, (1, 2)), jnp.tile(sin_t, (1, 2))
    qk = lambda t, w, cs, sn: _rope(_headnorm64(t, w), cs[:, :LANES], sn[:, :LANES])
    emit(_slabs(lambda t, w, cs, sn: qk(t, w, cs, sn) * scale, z_aq, aqw_ref[...], cos2, sin2,
                width=AT_WIDTH), q1_o, q4_o, q16_o)
    emit(_slabs(qk, z_ak, akw_ref[...], cos2, sin2, width=AT_WIDTH), k1_o, k4_o, k16_o)
    emit(z_av, v1_o, v4_o, v16_o)
    mq_o[...] = _slabs(lambda t, w: (_headnorm64(t, w) * scale).astype(BF16), z_mq, mqw_ref[...],
                       width=MEM_WIDTH)
    for c in chunks:
        for hd in heads:
            o_sum = o_parts[c][hd][0]
            for part in o_parts[c][hd][1:]:
                o_sum = o_sum + part
            osum_o[row_sl[c], col_sl[hd]] = o_sum.astype(BF16)


def _front(x, nw, w_in, lbf, lbb, aqw, akw, mqw, cos_t, sin_t):
    B, S, _ = x.shape
    ts = PROJ_TILE
    nt = S // ts
    per_att = ATT_TILE // ts
    row = lambda n: pl.BlockSpec((None, ts, n), lambda b, i: (b, i, 0))
    vec = lambda n: pl.BlockSpec((1, n), lambda b, i: (0, 0))
    tab = pl.BlockSpec((ts, LANES), lambda b, i: (i, 0))

    def grouped(dil):
        spec = pl.BlockSpec((None, None, dil, ts // dil, AT_WIDTH),
                            lambda b, i: (b, i // per_att, 0, i % per_att, 0))
        shape = jax.ShapeDtypeStruct((B, S // ATT_TILE, dil, ATT_TILE // dil, AT_WIDTH), BF16)
        return spec, shape

    outs = [(row(HG_WIDTH), jax.ShapeDtypeStruct((B, S, HG_WIDTH), BF16)),
            (row(HG_WIDTH), jax.ShapeDtypeStruct((B, S, HG_WIDTH), BF16)),
            (pl.BlockSpec((None, None, HG_HEADS, HG_D, HG_D), lambda b, i: (b, i, 0, 0, 0)),
             jax.ShapeDtypeStruct((B, nt, HG_HEADS, HG_D, HG_D), F32)),
            (pl.BlockSpec((None, None, 1, HG_WIDTH), lambda b, i: (b, i, 0, 0)),
             jax.ShapeDtypeStruct((B, nt, 1, HG_WIDTH), F32))]
    for _ in range(3):
        outs.append((row(AT_WIDTH), jax.ShapeDtypeStruct((B, S, AT_WIDTH), BF16)))
        outs.append(grouped(4))
        outs.append(grouped(16))
    outs.append((row(MEM_WIDTH), jax.ShapeDtypeStruct((B, S, MEM_WIDTH), BF16)))
    return pl.pallas_call(
        _front_kernel,
        grid=(B, nt),
        in_specs=[row(D_MODEL), vec(D_MODEL),
                  pl.BlockSpec((D_MODEL, FRONT_WIDTH), lambda b, i: (0, 0)),
                  vec(HG_WIDTH), vec(HG_WIDTH), vec(AT_WIDTH), vec(AT_WIDTH), vec(MEM_WIDTH),
                  tab, tab],
        out_specs=[s for s, _ in outs],
        out_shape=[s for _, s in outs],
        scratch_shapes=[pltpu.VMEM((HG_HEADS, HG_D, HG_D), F32)]
                       + [pltpu.VMEM((ts, LANES), F32)] * 2
                       + [pltpu.VMEM((4, ts // 4, LANES), F32)] * 2,
        compiler_params=_cparams("parallel", "arbitrary"),
        name="front",
    )(x, nw, w_in, lbf, lbb, aqw, akw, mqw, cos_t, sin_t)


def _mem_kv_kernel(m_ref, nw_ref, w_ref, kw_ref, mk_o, mva_o, mvb_o):
    h = _rms_norm_bf16(m_ref[...], nw_ref[...])
    kv = jnp.dot(h, w_ref[...], preferred_element_type=F32)
    mk_o[...] = _headnorm64(kv[:, :MEM_WIDTH], kw_ref[...]).astype(BF16)
    lo_mask = (lax.broadcasted_iota(jnp.int32, (1, MEM_WIDTH), 1) % LANES) < HEAD64
    mv = kv[:, MEM_WIDTH:]
    mva_o[...] = jnp.where(lo_mask, mv, 1.0).astype(BF16)
    mvb_o[...] = jnp.where(lo_mask, 1.0, mv).astype(BF16)


def _mem_kv(mem, nw, wkv, kw):
    B, M, _ = mem.shape
    out = pl.BlockSpec((None, M, MEM_WIDTH), lambda b: (b, 0, 0))
    return pl.pallas_call(
        _mem_kv_kernel,
        grid=(B,),
        in_specs=[pl.BlockSpec((None, M, D_MODEL), lambda b: (b, 0, 0)),
                  pl.BlockSpec((1, D_MODEL), lambda b: (0, 0)),
                  pl.BlockSpec((D_MODEL, 2 * MEM_WIDTH), lambda b: (0, 0)),
                  pl.BlockSpec((1, MEM_WIDTH), lambda b: (0, 0))],
        out_specs=[out, out, out],
        out_shape=[jax.ShapeDtypeStruct((B, M, MEM_WIDTH), BF16)] * 3,
        compiler_params=_cparams("parallel"),
        name="mem_kv",
    )(mem, nw, wkv, kw)


def _attn_kernel(negm_ref, q1_ref, q4_ref, q16_ref,
                 k1p, k1m, k1n, k4p, k4m, k4n, k16p, k16m, k16n,
                 v1p, v1m, v1n, v4p, v4m, v4n, v16p, v16m, v16n,
                 o_ref, num_lo, num_hi, den_lo, den_hi, bias_s, *, nt):
    num_s = (num_lo, num_hi)
    den_s = (den_lo, den_hi)
    i = pl.program_id(1)
    lo_mask = _lane_iota() < HEAD64
    WIN = QBLK + 2 * BAND

    qi = lax.broadcasted_iota(jnp.int32, (QBLK, WIN), 0)
    ke = lax.broadcasted_iota(jnp.int32, (QBLK, WIN), 1)
    delta = ke - BAND - qi
    band = (delta <= BAND) & (delta >= -BAND)
    left_ok = (ke >= BAND) | (i > 0)
    right_ok = (ke < QBLK + BAND) | (i < nt - 1)
    for kind, ok in enumerate((band, band & left_ok, band & right_ok, band & left_ok & right_ok)):
        bias_s[kind] = jnp.where(ok, negm_ref[...], MASK_VALUE)

    def pattern(dil, q_ref, kp, km, kn, vp, vm, vn, assign):
        rows = ATT_TILE // dil
        blocks_per_res = rows // QBLK
        n_groups = dil * blocks_per_res // ATT_GROUP

        def window(p_ref, m_ref, n_ref, res, blk):
            a = blk * QBLK
            parts = [p_ref[res]] if blk == 0 else [m_ref[res, a - BAND:a, :]]
            parts.append(m_ref[res, a:a + QBLK, :])
            parts.append(n_ref[res] if blk == blocks_per_res - 1
                         else m_ref[res, a + QBLK:a + QBLK + BAND, :])
            return jnp.concatenate(parts, axis=0)

        def group(grp, res_of, blk_of):
            pairs, dests = [], []
            for u in range(ATT_GROUP):
                res, blk = res_of(grp, u), blk_of(u)
                a = blk * QBLK
                kind = (1 if blk == 0 else 0) + (2 if blk == blocks_per_res - 1 else 0)
                k_win = window(kp, km, kn, res, blk)
                v_win = window(vp, vm, vn, res, blk)
                out_rows = pl.ds(a, QBLK) if dil == 1 else pl.ds(a * dil + res, QBLK, stride=dil)
                for hp in range(AT_WIDTH // LANES):
                    cols = slice(hp * LANES, (hp + 1) * LANES)
                    v2 = v_win[:, cols]
                    one = jnp.ones_like(v2)
                    pairs.append((q_ref[res, a:a + QBLK, cols], k_win[:, cols],
                                  jnp.where(lo_mask, v2, one), jnp.where(lo_mask, one, v2),
                                  bias_s[kind]))
                    dests.append((hp, out_rows))
            for (hp, out_rows), (num, den) in zip(dests, _pairs_attend(pairs)):
                if assign:
                    num_s[hp][out_rows, :] = num
                    den_s[hp][out_rows, :] = den
                else:
                    num_s[hp][out_rows, :] += num
                    den_s[hp][out_rows, :] += den

        if blocks_per_res >= ATT_GROUP:
            per_res = blocks_per_res // ATT_GROUP
            for res in range(dil):
                for g in range(per_res):
                    group(0, lambda grp, u, res=res: res, lambda u, g=g: g * ATT_GROUP + u)
        else:
            res_per_group = ATT_GROUP // blocks_per_res

            def body(grp, carry):
                group(grp, lambda grp, u: grp * res_per_group + u // blocks_per_res,
                      lambda u: u % blocks_per_res)
                return carry

            lax.fori_loop(0, n_groups, body, 0)

    pattern(16, q16_ref, k16p, k16m, k16n, v16p, v16m, v16n, True)
    pattern(4, q4_ref, k4p, k4m, k4n, v4p, v4m, v4n, False)
    pattern(1, q1_ref, k1p, k1m, k1n, v1p, v1m, v1n, False)
    for hp in range(AT_WIDTH // LANES):
        o_ref[:, hp * LANES:(hp + 1) * LANES] = (num_s[hp][...] / den_s[hp][...]).astype(BF16)


def _attention(negm, q, k, v):
    B, S, _ = q[0].shape
    nt = S // ATT_TILE

    def view(t, dil):
        return t.reshape(B, nt, dil, ATT_TILE // dil, AT_WIDTH)

    def main(dil):
        return pl.BlockSpec((None, None, dil, ATT_TILE // dil, AT_WIDTH),
                            lambda b, i: (b, i, 0, 0, 0))

    def prev(dil):
        lastblk = ATT_TILE // dil // BAND - 1
        return pl.BlockSpec((None, None, dil, BAND, AT_WIDTH),
                            lambda b, i: (b, jnp.maximum(i - 1, 0), 0, lastblk, 0))

    def nxt(dil):
        return pl.BlockSpec((None, None, dil, BAND, AT_WIDTH),
                            lambda b, i: (b, jnp.minimum(i + 1, nt - 1), 0, 0, 0))

    qs = [view(t, d) for t, d in zip(q, DILATIONS)]
    ks = [view(t, d) for t, d in zip(k, DILATIONS)]
    vs = [view(t, d) for t, d in zip(v, DILATIONS)]
    halo_specs, halo_args = [], []
    for ts_ in (ks, vs):
        for t, d in zip(ts_, DILATIONS):
            halo_specs += [prev(d), main(d), nxt(d)]
            halo_args += [t, t, t]
    return pl.pallas_call(
        functools.partial(_attn_kernel, nt=nt),
        grid=(B, nt),
        in_specs=[pl.BlockSpec((1, QBLK + 2 * BAND), lambda b, i: (0, 0))]
                 + [main(d) for d in DILATIONS] + halo_specs,
        out_specs=pl.BlockSpec((None, ATT_TILE, AT_WIDTH), lambda b, i: (b, i, 0)),
        out_shape=jax.ShapeDtypeStruct((B, S, AT_WIDTH), BF16),
        scratch_shapes=[pltpu.VMEM((ATT_TILE, LANES), F32)] * 4
                       + [pltpu.VMEM((4, QBLK, QBLK + 2 * BAND), F32)],
        compiler_params=_cparams("parallel", "parallel"),
        name="dilated_attn",
    )(negm, *qs, *halo_args)


def _out_kernel(x_ref, nw_ref, wg_ref, osum_ref, qtb_ref, lst_ref, dtile_ref, oat_ref, mq_ref,
                mk_ref, mva_ref, mvb_ref, negm_ref, onw_ref, w_ref, y_ref, sin_ref):
    nt_dims = (((1,), (1,)), ((), ()))

    @pl.when(pl.program_id(1) == 0)
    def _():
        sin_ref[...] = jnp.zeros_like(sin_ref)

    h = _rms_norm_bf16(x_ref[...], nw_ref[...])
    z_g = jnp.dot(h, wg_ref[...], preferred_element_type=F32)
    mem = [_pair_attend(mq_ref[:, cols], mk_ref[:, cols], mva_ref[:, cols], mvb_ref[:, cols],
                        negm_ref[...])
           for cols in (slice(0, LANES), slice(LANES, 2 * LANES))]
    n_sub = lst_ref.shape[0]
    corr = []
    for hd in range(HG_HEADS):
        cols = slice(hd * HG_D, (hd + 1) * HG_D)
        st = sin_ref[hd]
        pieces = [None] * n_sub
        for sub in reversed(range(n_sub)):
            rows = slice(sub * PROJ_TILE, (sub + 1) * PROJ_TILE)
            pieces[sub] = lax.dot_general(qtb_ref[rows, cols], st.astype(BF16), nt_dims,
                                          preferred_element_type=F32)
            st = st * dtile_ref[sub][:, cols] + lst_ref[sub, hd]
        sin_ref[hd] = st
        corr.append(jnp.concatenate(pieces, axis=0))
    parts = []

    def hgrn_out(o_sum, cr, zg, w):
        t = o_sum.astype(F32) + cr
        ms = jnp.mean(t * t, axis=-1, keepdims=True)
        return (t * lax.rsqrt(ms + NORM_EPS) * w * (zg * _sigmoid(zg))).astype(BF16)

    for hd in range(HG_HEADS):
        cols = slice(hd * HG_D, (hd + 1) * HG_D)
        parts.append(_slabs(hgrn_out, osum_ref[:, cols], corr[hd], z_g[:, cols], onw_ref[...]))
    c0 = HG_WIDTH
    parts.append(_slabs(lambda o, zg: (o.astype(F32) * (zg * _sigmoid(zg))).astype(BF16),
                        oat_ref[...], z_g[:, c0:c0 + AT_WIDTH]))
    c0 += AT_WIDTH
    for hp, (num, den) in enumerate(mem):
        parts.append(_slabs(lambda n, d, zg: (n / d * (zg * _sigmoid(zg))).astype(BF16),
                            num, den, z_g[:, c0 + hp * LANES:c0 + (hp + 1) * LANES]))
    mixed = jnp.concatenate(parts, axis=-1)
    y_ref[...] = x_ref[...] + jnp.dot(mixed, w_ref[...], preferred_element_type=F32)


def _out_stage(x, nw, w_in, o_sum, qtb, lst, dtile, o_at, mq, mk, mva, mvb, negm, onw, w_out):
    B, S, _ = x.shape
    ts = OUT_TILE
    nt = S // ts
    n_sub = OUT_TILE // PROJ_TILE
    row = lambda n: pl.BlockSpec((None, ts, n), lambda b, i: (b, nt - 1 - i, 0))
    memspec = pl.BlockSpec((None, MEM_TOKENS, MEM_WIDTH), lambda b, i: (b, 0, 0))
    return pl.pallas_call(
        _out_kernel,
        grid=(B, nt),
        in_specs=[row(D_MODEL), pl.BlockSpec((1, D_MODEL), lambda b, i: (0, 0)),
                  pl.BlockSpec((D_MODEL, IN_WIDTH - C_GATE),
                               lambda b, i: (0, C_GATE // (IN_WIDTH - C_GATE))),
                  row(HG_WIDTH), row(HG_WIDTH),
                  pl.BlockSpec((None, n_sub, HG_HEADS, HG_D, HG_D),
                               lambda b, i: (b, nt - 1 - i, 0, 0, 0)),
                  pl.BlockSpec((None, n_sub, 1, HG_WIDTH), lambda b, i: (b, nt - 1 - i, 0, 0)),
                  row(AT_WIDTH), row(MEM_WIDTH), memspec, memspec, memspec,
                  pl.BlockSpec((1, MEM_TOKENS), lambda b, i: (0, 0)),
                  pl.BlockSpec((1, HG_D), lambda b, i: (0, 0)),
                  pl.BlockSpec((D_MODEL, D_MODEL), lambda b, i: (0, 0))],
        out_specs=row(D_MODEL),
        out_shape=jax.ShapeDtypeStruct((B, S, D_MODEL), F32),
        scratch_shapes=[pltpu.VMEM((HG_HEADS, HG_D, HG_D), F32)],
        compiler_params=_cparams("parallel", "arbitrary"),
        name="out_stage",
    )(x, nw, w_in, o_sum, qtb, lst, dtile, o_at, mq, mk, mva, mvb, negm, onw, w_out)


def _rope_tables(S):
    half = ROPE_DIM // 2
    pos = jnp.arange(S, dtype=F32)
    inv_freq = ROPE_THETA ** (-jnp.arange(half, dtype=F32) * 2.0 / ROPE_DIM)
    ang = pos[:, None] * inv_freq[None, :]
    cos, sin = jnp.cos(ang), jnp.sin(ang)
    pad = HEAD64 - ROPE_DIM
    cos64 = jnp.concatenate([cos, cos, jnp.ones((S, pad), F32)], axis=-1)
    sin64 = jnp.concatenate([-sin, sin, jnp.zeros((S, pad), F32)], axis=-1)
    return jnp.tile(cos64, (1, 2)), jnp.tile(sin64, (1, 2))


def _lower_bounds(p):
    sm = jax.nn.softmax(p.astype(F32), axis=0)
    return jnp.cumsum(sm, axis=0) - sm[0:1]


def _tile4(w):
    return jnp.tile(w.astype(F32), (1, 4))


def _neg_stabiliser(wq, wk, width):
    bound = (HEAD64 ** 0.5 * LOG2E) * jnp.max(jnp.abs(wq), axis=-1) * jnp.max(jnp.abs(wk), axis=-1)
    bound = jnp.minimum(bound.astype(F32), EXP2_RANGE)
    return jnp.broadcast_to(-bound[:, None, None], (DEPTH, 1, width))


def _trunk(x, mem, prm):
    B, S, _ = x.shape
    assert S % ATT_TILE == 0 and ATT_TILE % PROJ_TILE == 0 and PROJ_TILE % SCAN_CHUNK == 0
    assert S % OUT_TILE == 0 and OUT_TILE % PROJ_TILE == 0
    cos_t, sin_t = _rope_tables(S)
    for l in range(DEPTH):
        p = {k: v[l] for k, v in prm.items()}
        (o_sum, qtb, lst, dtile, q1, q4, q16, k1, k4, k16, v1, v4, v16, mq) = _front(
            x, p["norm_w"], p["w_in"], p["lbf"], p["lbb"], p["aqw"], p["akw"], p["mqw"],
            cos_t, sin_t)
        mk, mva, mvb = _mem_kv(mem, p["mem_norm_w"], p["mem_wkv"], p["mkw"])
        o_at = _attention(p["negm_at"], (q1, q4, q16), (k1, k4, k16), (v1, v4, v16))
        x = _out_stage(x, p["norm_w"], p["w_in"], o_sum, qtb, lst, dtile, o_at, mq, mk, mva, mvb,
                       p["negm_mem"], p["onw"], p["w_out"])
    return x


def kernel(x_prompt, x_sample, mem_prompt, mem_sample, norm_w, w_in, hgrn_lb_fwd, hgrn_lb_bwd,
           hgrn_onorm_w, attn_qnorm_w, attn_knorm_w, mem_norm_w, mem_wkv, mem_qnorm_w,
           mem_knorm_w, w_out):
    prm = {
        "norm_w": norm_w.astype(F32)[:, None, :],
        "w_in": w_in.astype(BF16),
        "lbf": _lower_bounds(hgrn_lb_fwd)[:, None, :],
        "lbb": _lower_bounds(hgrn_lb_bwd)[:, None, :],
        "onw": hgrn_onorm_w.astype(F32)[:, None, :],
        "aqw": _tile4(attn_qnorm_w)[:, None, :],
        "akw": _tile4(attn_knorm_w)[:, None, :],
        "mem_norm_w": mem_norm_w.astype(F32)[:, None, :],
        "mem_wkv": mem_wkv.astype(BF16),
        "mqw": _tile4(mem_qnorm_w)[:, None, :],
        "mkw": _tile4(mem_knorm_w)[:, None, :],
        "w_out": w_out.astype(BF16),
        "negm_at": _neg_stabiliser(attn_qnorm_w, attn_knorm_w, QBLK + 2 * BAND),
        "negm_mem": _neg_stabiliser(mem_qnorm_w, mem_knorm_w, MEM_TOKENS),
    }
    y_prompt = _trunk(x_prompt, mem_prompt, prm)
    y_sample = _trunk(x_sample, mem_sample, prm)
    return (y_prompt, y_sample)
```

```python
import functools

import jax
import jax.numpy as jnp
from jax import lax
from jax.experimental import pallas as pl
from jax.experimental.pallas import tpu as pltpu

F32 = jnp.float32
BF16 = jnp.bfloat16

D_MODEL = 1024
DEPTH = 4
HG_HEADS = 4
HG_D = 128
HG_WIDTH = HG_HEADS * HG_D
AT_WIDTH = 256
HEAD64 = 64
MEM_WIDTH = 256
MEM_TOKENS = 256
IN_WIDTH = 4096
DILATIONS = (1, 4, 16)
BAND = 64
ROPE_THETA = 500000.0
ROPE_DIM = 16
NORM_EPS = 1e-6
MASK_VALUE = -1e30
LANES = 128
SLAB_ROWS = 32

C_HQ, C_FF, C_FB, C_HI = 0, 512, 1024, 1536
C_AQ, C_AK, C_AV, C_MQ = 2048, 2304, 2560, 2816
C_GATE = 3072
FRONT_WIDTH = C_GATE

LOG2E = 1.4426950408889634
EXP2_RANGE = 63.0
PROJ_TILE = 1024
OUT_TILE = 1024
ROW_BLK = 256
ATT_TILE = 2048
QBLK = 128
ATT_GROUP = 16
SCAN_CHUNK = 64
EXP2_CLAMP = 115.0
VMEM_LIMIT = 56 * 1024 * 1024


def _cparams(*sem):
    return pltpu.CompilerParams(dimension_semantics=sem, vmem_limit_bytes=VMEM_LIMIT)


def _sigmoid(z):
    return 1.0 / (1.0 + jnp.exp(-z))


def _lane_iota(n=LANES):
    return lax.broadcasted_iota(jnp.int32, (1, n), 1)


def _slabs(fn, *xs, rows=SLAB_ROWS, width=LANES):
    n_rows = max(x.shape[0] for x in xs)
    n_cols = xs[0].shape[1]
    row_blocks = []
    for r0 in range(0, n_rows, rows):
        col_blocks = []
        for c0 in range(0, n_cols, width):
            res = fn(*[x[(slice(None) if x.shape[0] == 1 else slice(r0, r0 + rows)), c0:c0 + width]
                       for x in xs])
            col_blocks.append(res if isinstance(res, tuple) else (res,))
        row_blocks.append([jnp.concatenate(parts, axis=1) if len(parts) > 1 else parts[0]
                           for parts in zip(*col_blocks)])
    outs = [jnp.concatenate(parts, axis=0) if len(parts) > 1 else parts[0]
            for parts in zip(*row_blocks)]
    return outs[0] if len(outs) == 1 else tuple(outs)


def _rms_norm_bf16(x, w):
    def fn(xs, ws):
        ms = jnp.mean(xs * xs, axis=-1, keepdims=True)
        return (xs * lax.rsqrt(ms + NORM_EPS) * ws).astype(BF16)
    return _slabs(fn, x, w, width=x.shape[1])


def _silu(z):
    return _slabs(lambda t: t * _sigmoid(t), z)


def _headnorm64(t, w):
    lo_mask = _lane_iota() < HEAD64
    parts = []
    for hp in range(t.shape[1] // LANES):
        th = t[:, hp * LANES:(hp + 1) * LANES]
        t2 = th * th
        lo = jnp.sum(jnp.where(lo_mask, t2, 0.0), axis=-1, keepdims=True)
        hi = jnp.sum(jnp.where(lo_mask, 0.0, t2), axis=-1, keepdims=True)
        ms = jnp.where(lo_mask, lo, hi) * (1.0 / HEAD64)
        parts.append(th * lax.rsqrt(ms + NORM_EPS))
    return jnp.concatenate(parts, axis=-1) * w


def _rope(t, cos_t, sin_t):
    lane = _lane_iota() % HEAD64
    parts = []
    for hp in range(t.shape[1] // LANES):
        th = t[:, hp * LANES:(hp + 1) * LANES]
        partner = jnp.where(lane < ROPE_DIM // 2,
                            pltpu.roll(th, LANES - ROPE_DIM // 2, 1),
                            pltpu.roll(th, ROPE_DIM // 2, 1))
        parts.append(th * cos_t + partner * sin_t)
    return jnp.concatenate(parts, axis=-1)


def _pair_attend(q2, k2, va, vb, bias):
    return _pairs_attend([(q2, k2, va, vb, bias)])[0]


def _pairs_attend(pairs):
    lo_mask = _lane_iota() < HEAD64
    nt = (((1,), (1,)), ((), ()))
    s = []
    for q2, k2, _, _, _ in pairs:
        zero = jnp.zeros_like(q2)
        s.append((lax.dot_general(jnp.where(lo_mask, q2, zero), k2, nt, preferred_element_type=F32),
                  lax.dot_general(jnp.where(lo_mask, zero, q2), k2, nt, preferred_element_type=F32)))
    prob = lambda sc, bias: jnp.exp2(sc + bias).astype(BF16)
    p = [(_slabs(prob, s_lo, pr[4], width=s_lo.shape[1]), _slabs(prob, s_hi, pr[4], width=s_hi.shape[1]))
         for (s_lo, s_hi), pr in zip(s, pairs)]
    o = [(jnp.dot(p_lo, pr[2], preferred_element_type=F32),
          jnp.dot(p_hi, pr[3], preferred_element_type=F32)) for (p_lo, p_hi), pr in zip(p, pairs)]
    return [(jnp.where(lo_mask, o_lo, o_hi), pltpu.roll(jnp.where(lo_mask, o_hi, o_lo), HEAD64, 1))
            for o_lo, o_hi in o]


def _front_kernel(x_ref, nw_ref, w_ref, lbf_ref, lbb_ref, aqw_ref, akw_ref, mqw_ref,
                  cos_ref, sin_ref,
                  osum_o, qtb_o, lst_o, dtile_o,
                  q1_o, q4_o, q16_o, k1_o, k4_o, k16_o, v1_o, v4_o, v16_o, mq_o,
                  st_ref, tmp_lo, tmp_hi, tmp4_lo, tmp4_hi):
    ts = x_ref.shape[0]
    C = SCAN_CHUNK
    chunks = range(ts // C)
    heads = range(HG_HEADS)
    nt_dims = (((1,), (1,)), ((), ()))
    tn_dims = (((0,), (0,)), ((), ()))
    row_sl = [slice(c * C, (c + 1) * C) for c in chunks]
    col_sl = [slice(hd * HG_D, (hd + 1) * HG_D) for hd in heads]

    @pl.when(pl.program_id(1) == 0)
    def _():
        st_ref[...] = jnp.zeros_like(st_ref)

    h = _rms_norm_bf16(x_ref[...], nw_ref[...])

    def proj(c0, n):
        return jnp.dot(h, w_ref[:, c0:c0 + n], preferred_element_type=F32)

    def gate_terms(z, lb_ref):
        def fn(zs, lb):
            sig = _sigmoid(zs)
            lf = jnp.log2(lb + (1.0 - lb) * sig)
            lf_hi = lf.astype(BF16)
            return lf_hi, (lf - lf_hi.astype(F32)).astype(BF16), (1.0 - lb) * (1.0 - sig)
        return _slabs(fn, z, lb_ref[...])

    def proj_rows(r0, c0, n):
        return jnp.dot(h[r0:r0 + ROW_BLK, :], w_ref[:, c0:c0 + n], preferred_element_type=F32)

    acc = {k: [] for k in ("q", "v", "ff", "fb", "aq", "ak")}
    for r0 in range(0, ts, ROW_BLK):
        z_q = proj_rows(r0, C_HQ, HG_WIDTH)
        z_i = proj_rows(r0, C_HI, HG_WIDTH)
        acc["q"].append(_silu(z_q))
        z_f = proj_rows(r0, C_FF, HG_WIDTH)
        acc["v"].append(z_i.astype(BF16))
        z_b = proj_rows(r0, C_FB, HG_WIDTH)
        acc["ff"].append(gate_terms(z_f, lbf_ref))
        acc["aq"].append(proj_rows(r0, C_AQ, AT_WIDTH))
        acc["ak"].append(proj_rows(r0, C_AK, AT_WIDTH))
        acc["fb"].append(gate_terms(z_b, lbb_ref))
    cat = lambda parts: jnp.concatenate(parts, axis=0)
    q, v_bf, z_aq, z_ak = cat(acc["q"]), cat(acc["v"]), cat(acc["aq"]), cat(acc["ak"])
    lfh_f, lfl_f, kk_f = (cat(parts) for parts in zip(*acc["ff"]))
    lfh_b, lfl_b, kk_b = (cat(parts) for parts in zip(*acc["fb"]))

    ri = lax.broadcasted_iota(jnp.int32, (C, C), 0)
    ci = lax.broadcasted_iota(jnp.int32, (C, C), 1)
    tri = [ci <= ri, ci >= ri]
    tri_bf = [jnp.tile(jnp.where(t, 1.0, 0.0).astype(BF16), (1, 2)) for t in tri]
    end_row = [C - 1, 0]
    mid_row = [C // 2 - 1, C // 2]
    lfh_d, lfl_d, kk_d = [lfh_f, lfh_b], [lfl_f, lfl_b], [kk_f, kk_b]
    dirs = range(2)

    b_all = [[None] * len(chunks) for _ in dirs]
    for d in dirs:
        for c in chunks:
            lf2 = jnp.concatenate([lfh_d[d][row_sl[c], :], lfl_d[d][row_sl[c], :]], axis=0)
            b_all[d][c] = jnp.dot(tri_bf[d], lf2, preferred_element_type=F32)
    z_av = proj(C_AV, AT_WIDTH)
    z_mq = proj(C_MQ, MEM_WIDTH)

    q_inter = [[None] * len(chunks) for _ in dirs]
    k_state = [[None] * len(chunks) for _ in dirs]
    q_intra = [[None] * len(chunks) for _ in dirs]
    k_intra = [[None] * len(chunks) for _ in dirs]
    dec = [[None] * len(chunks) for _ in dirs]
    off = jnp.zeros((1, HG_WIDTH), F32)
    tail = [None] * len(chunks)
    for c in reversed(chunks):
        tail[c] = jnp.exp2(off)
        off = off + b_all[1][c][end_row[1]:end_row[1] + 1, :]
    dtile_o[...] = jnp.exp2(off)

    def operands(bs, qs, ks, be, bm, *tl):
        qi = qs * jnp.exp2(bs)
        out = (qi.astype(BF16),
               (ks * jnp.exp2(be - bs)).astype(BF16),
               (qs * jnp.exp2(jnp.minimum(bs - bm, EXP2_CLAMP))).astype(BF16),
               (ks * jnp.exp2(jnp.minimum(bm - bs, EXP2_CLAMP))).astype(BF16))
        return out + tuple((qi * t).astype(BF16) for t in tl)

    for d in dirs:
        for c in chunks:
            b = b_all[d][c]
            b_end = b[end_row[d]:end_row[d] + 1, :]
            b_mid = b[mid_row[d]:mid_row[d] + 1, :]
            dec[d][c] = jnp.exp2(b_end)
            res = _slabs(operands, b, q[row_sl[c], :], kk_d[d][row_sl[c], :], b_end, b_mid,
                         *([tail[c]] if d == 1 else []))
            q_inter[d][c], k_state[d][c], q_intra[d][c], k_intra[d][c] = res[:4]
            if d == 1:
                qtb_o[row_sl[c], :] = res[4]

    zeros_k = jnp.zeros((C, HG_D), BF16)

    def both_scores(c, hd):
        lhs = jnp.concatenate([q_intra[d][c][:, col_sl[hd]] for d in dirs], axis=1)
        rhs = jnp.concatenate(
            [jnp.concatenate([k_intra[0][c][:, col_sl[hd]], zeros_k], axis=1),
             jnp.concatenate([zeros_k, k_intra[1][c][:, col_sl[hd]]], axis=1)], axis=0)
        return lax.dot_general(lhs, rhs, nt_dims, preferred_element_type=F32)

    sc = [[both_scores(c, hd) for hd in heads] for c in chunks]
    upd2 = [[lax.dot_general(v_bf[row_sl[c], col_sl[hd]],
                             jnp.concatenate([k_state[d][c][:, col_sl[hd]] for d in dirs], axis=1),
                             tn_dims, preferred_element_type=F32) for hd in heads] for c in chunks]
    upd = [[[upd2[c][hd][:, d * HG_D:(d + 1) * HG_D] for hd in heads] for c in chunks]
           for d in dirs]
    ri2 = lax.broadcasted_iota(jnp.int32, (C, 2 * C), 0)
    ci2 = lax.broadcasted_iota(jnp.int32, (C, 2 * C), 1)
    tri2 = (ci2 <= ri2) | (ci2 - C >= ri2)
    sc = [[jnp.where(tri2, sc[c][hd], 0.0).astype(BF16) for hd in heads] for c in chunks]

    st_in = [[[None] * HG_HEADS for _ in chunks] for _ in dirs]
    for hd in heads:
        def forward(st, *terms):
            seen = []
            for c in chunks:
                seen.append(st.astype(BF16))
                st = st * terms[2 * c] + terms[2 * c + 1]
            return (st, *seen)

        def backward(*terms):
            seen, st = [], None
            for c in reversed(chunks):
                if st is None:
                    st = terms[2 * c + 1]
                else:
                    seen.append(st.astype(BF16))
                    st = st * terms[2 * c] + terms[2 * c + 1]
            return (st, *seen)

        terms = [[t for c in chunks for t in (dec[d][c][:, col_sl[hd]], upd[d][c][hd])]
                 for d in dirs]
        st_new, *seen = _slabs(forward, st_ref[hd], *terms[0])
        st_ref[hd] = st_new
        for c in chunks:
            st_in[0][c][hd] = seen[c]
        st_new, *seen = _slabs(backward, *terms[1])
        lst_o[hd] = st_new
        for j, c in enumerate(reversed(chunks[:-1])):
            st_in[1][c][hd] = seen[j]
    zeros_st = jnp.zeros((HG_D, HG_D), BF16)
    o_parts = [[[] for _ in heads] for _ in chunks]
    for c in chunks:
        for hd in heads:
            v_c = v_bf[row_sl[c], col_sl[hd]]
            o_parts[c][hd].append(jnp.dot(sc[c][hd], jnp.concatenate([v_c, v_c], axis=0),
                                          preferred_element_type=F32))
            states = [zeros_st if st_in[d][c][hd] is None else st_in[d][c][hd] for d in dirs]
            o_parts[c][hd].append(lax.dot_general(
                jnp.concatenate([q_inter[d][c][:, col_sl[hd]] for d in dirs], axis=1),
                jnp.concatenate(states, axis=1), nt_dims, preferred_element_type=F32))

    def emit(val, o1, o4, o16):
        o1[...] = val.astype(BF16)
        for hp, (tmp, tmp4) in enumerate(((tmp_lo, tmp4_lo), (tmp_hi, tmp4_hi))):
            cols = slice(hp * LANES, (hp + 1) * LANES)
            tmp[...] = val[:, cols]
            for r4 in range(4):
                cls = tmp[pl.ds(r4, ts // 4, stride=4), :]
                o4[r4, :, cols] = cls.astype(BF16)
                tmp4[r4] = cls
                for j in range(4):
                    o16[r4 + 4 * j, :, cols] = tmp4[r4, pl.ds(j, ts // 16, stride=4), :].astype(BF16)

    cos_t = cos_ref[...]
    sin_t = sin_ref[...]
    scale = LOG2E * HEAD64 ** -0.5
    cos2, sin2 = jnp.tile(cos_t, (1, 2)), jnp.tile(sin_t, (1, 2))
    qk = lambda t, w, cs, sn: _rope(_headnorm64(t, w), cs[:, :LANES], sn[:, :LANES])
    emit(_slabs(lambda t, w, cs, sn: qk(t, w, cs, sn) * scale, z_aq, aqw_ref[...], cos2, sin2,
                width=AT_WIDTH), q1_o, q4_o, q16_o)
    emit(_slabs(qk, z_ak, akw_ref[...], cos2, sin2, width=AT_WIDTH), k1_o, k4_o, k16_o)
    emit(z_av, v1_o, v4_o, v16_o)
    mq_o[...] = _slabs(lambda t, w: (_headnorm64(t, w) * scale).astype(BF16), z_mq, mqw_ref[...],
                       width=MEM_WIDTH)
    for c in chunks:
        for hd in heads:
            o_sum = o_parts[c][hd][0]
            for part in o_parts[c][hd][1:]:
                o_sum = o_sum + part
            osum_o[row_sl[c], col_sl[hd]] = o_sum.astype(BF16)


def _front(x, nw, w_in, lbf, lbb, aqw, akw, mqw, cos_t, sin_t):
    B, S, _ = x.shape
    ts = PROJ_TILE
    nt = S // ts
    per_att = ATT_TILE // ts
    row = lambda n: pl.BlockSpec((None, ts, n), lambda b, i: (b, i, 0))
    vec = lambda n: pl.BlockSpec((1, n), lambda b, i: (0, 0))
    tab = pl.BlockSpec((ts, LANES), lambda b, i: (i, 0))

    def grouped(dil):
        spec = pl.BlockSpec((None, None, dil, ts // dil, AT_WIDTH),
                            lambda b, i: (b, i // per_att, 0, i % per_att, 0))
        shape = jax.ShapeDtypeStruct((B, S // ATT_TILE, dil, ATT_TILE // dil, AT_WIDTH), BF16)
        return spec, shape

    outs = [(row(HG_WIDTH), jax.ShapeDtypeStruct((B, S, HG_WIDTH), BF16)),
            (row(HG_WIDTH), jax.ShapeDtypeStruct((B, S, HG_WIDTH), BF16)),
            (pl.BlockSpec((None, None, HG_HEADS, HG_D, HG_D), lambda b, i: (b, i, 0, 0, 0)),
             jax.ShapeDtypeStruct((B, nt, HG_HEADS, HG_D, HG_D), F32)),
            (pl.BlockSpec((None, None, 1, HG_WIDTH), lambda b, i: (b, i, 0, 0)),
             jax.ShapeDtypeStruct((B, nt, 1, HG_WIDTH), F32))]
    for _ in range(3):
        outs.append((row(AT_WIDTH), jax.ShapeDtypeStruct((B, S, AT_WIDTH), BF16)))
        outs.append(grouped(4))
        outs.append(grouped(16))
    outs.append((row(MEM_WIDTH), jax.ShapeDtypeStruct((B, S, MEM_WIDTH), BF16)))
    return pl.pallas_call(
        _front_kernel,
        grid=(B, nt),
        in_specs=[row(D_MODEL), vec(D_MODEL),
                  pl.BlockSpec((D_MODEL, FRONT_WIDTH), lambda b, i: (0, 0)),
                  vec(HG_WIDTH), vec(HG_WIDTH), vec(AT_WIDTH), vec(AT_WIDTH), vec(MEM_WIDTH),
                  tab, tab],
        out_specs=[s for s, _ in outs],
        out_shape=[s for _, s in outs],
        scratch_shapes=[pltpu.VMEM((HG_HEADS, HG_D, HG_D), F32)]
                       + [pltpu.VMEM((ts, LANES), F32)] * 2
                       + [pltpu.VMEM((4, ts // 4, LANES), F32)] * 2,
        compiler_params=_cparams("parallel", "arbitrary"),
        name="front",
    )(x, nw, w_in, lbf, lbb, aqw, akw, mqw, cos_t, sin_t)


def _mem_kv_kernel(m_ref, nw_ref, w_ref, kw_ref, mk_o, mva_o, mvb_o):
    h = _rms_norm_bf16(m_ref[...], nw_ref[...])
    kv = jnp.dot(h, w_ref[...], preferred_element_type=F32)
    mk_o[...] = _headnorm64(kv[:, :MEM_WIDTH], kw_ref[...]).astype(BF16)
    lo_mask = (lax.broadcasted_iota(jnp.int32, (1, MEM_WIDTH), 1) % LANES) < HEAD64
    mv = kv[:, MEM_WIDTH:]
    mva_o[...] = jnp.where(lo_mask, mv, 1.0).astype(BF16)
    mvb_o[...] = jnp.where(lo_mask, 1.0, mv).astype(BF16)


def _mem_kv(mem, nw, wkv, kw):
    B, M, _ = mem.shape
    out = pl.BlockSpec((None, M, MEM_WIDTH), lambda b: (b, 0, 0))
    return pl.pallas_call(
        _mem_kv_kernel,
        grid=(B,),
        in_specs=[pl.BlockSpec((None, M, D_MODEL), lambda b: (b, 0, 0)),
                  pl.BlockSpec((1, D_MODEL), lambda b: (0, 0)),
                  pl.BlockSpec((D_MODEL, 2 * MEM_WIDTH), lambda b: (0, 0)),
                  pl.BlockSpec((1, MEM_WIDTH), lambda b: (0, 0))],
        out_specs=[out, out, out],
        out_shape=[jax.ShapeDtypeStruct((B, M, MEM_WIDTH), BF16)] * 3,
        compiler_params=_cparams("parallel"),
        name="mem_kv",
    )(mem, nw, wkv, kw)


def _attn_kernel(negm_ref, q1_ref, q4_ref, q16_ref,
                 k1p, k1m, k1n, k4p, k4m, k4n, k16p, k16m, k16n,
                 v1p, v1m, v1n, v4p, v4m, v4n, v16p, v16m, v16n,
                 o_ref, num_lo, num_hi, den_lo, den_hi, bias_s, *, nt):
    num_s = (num_lo, num_hi)
    den_s = (den_lo, den_hi)
    i = pl.program_id(1)
    lo_mask = _lane_iota() < HEAD64
    WIN = QBLK + 2 * BAND

    qi = lax.broadcasted_iota(jnp.int32, (QBLK, WIN), 0)
    ke = lax.broadcasted_iota(jnp.int32, (QBLK, WIN), 1)
    delta = ke - BAND - qi
    band = (delta <= BAND) & (delta >= -BAND)
    left_ok = (ke >= BAND) | (i > 0)
    right_ok = (ke < QBLK + BAND) | (i < nt - 1)
    for kind, ok in enumerate((band, band & left_ok, band & right_ok, band & left_ok & right_ok)):
        bias_s[kind] = jnp.where(ok, negm_ref[...], MASK_VALUE)

    def pattern(dil, q_ref, kp, km, kn, vp, vm, vn, assign):
        rows = ATT_TILE // dil
        blocks_per_res = rows // QBLK
        n_groups = dil * blocks_per_res // ATT_GROUP

        def window(p_ref, m_ref, n_ref, res, blk):
            a = blk * QBLK
            parts = [p_ref[res]] if blk == 0 else [m_ref[res, a - BAND:a, :]]
            parts.append(m_ref[res, a:a + QBLK, :])
            parts.append(n_ref[res] if blk == blocks_per_res - 1
                         else m_ref[res, a + QBLK:a + QBLK + BAND, :])
            return jnp.concatenate(parts, axis=0)

        def group(grp, res_of, blk_of):
            pairs, dests = [], []
            for u in range(ATT_GROUP):
                res, blk = res_of(grp, u), blk_of(u)
                a = blk * QBLK
                kind = (1 if blk == 0 else 0) + (2 if blk == blocks_per_res - 1 else 0)
                k_win = window(kp, km, kn, res, blk)
                v_win = window(vp, vm, vn, res, blk)
                out_rows = pl.ds(a, QBLK) if dil == 1 else pl.ds(a * dil + res, QBLK, stride=dil)
                for hp in range(AT_WIDTH // LANES):
                    cols = slice(hp * LANES, (hp + 1) * LANES)
                    v2 = v_win[:, cols]
                    one = jnp.ones_like(v2)
                    pairs.append((q_ref[res, a:a + QBLK, cols], k_win[:, cols],
                                  jnp.where(lo_mask, v2, one), jnp.where(lo_mask, one, v2),
                                  bias_s[kind]))
                    dests.append((hp, out_rows))
            for (hp, out_rows), (num, den) in zip(dests, _pairs_attend(pairs)):
                if assign:
                    num_s[hp][out_rows, :] = num
                    den_s[hp][out_rows, :] = den
                else:
                    num_s[hp][out_rows, :] += num
                    den_s[hp][out_rows, :] += den

        if blocks_per_res >= ATT_GROUP:
            per_res = blocks_per_res // ATT_GROUP
            for res in range(dil):
                for g in range(per_res):
                    group(0, lambda grp, u, res=res: res, lambda u, g=g: g * ATT_GROUP + u)
        else:
            res_per_group = ATT_GROUP // blocks_per_res

            def body(grp, carry):
                group(grp, lambda grp, u: grp * res_per_group + u // blocks_per_res,
                      lambda u: u % blocks_per_res)
                return carry

            lax.fori_loop(0, n_groups, body, 0)

    pattern(16, q16_ref, k16p, k16m, k16n, v16p, v16m, v16n, True)
    pattern(4, q4_ref, k4p, k4m, k4n, v4p, v4m, v4n, False)
    pattern(1, q1_ref, k1p, k1m, k1n, v1p, v1m, v1n, False)
    for hp in range(AT_WIDTH // LANES):
        o_ref[:, hp * LANES:(hp + 1) * LANES] = (num_s[hp][...] / den_s[hp][...]).astype(BF16)


def _attention(negm, q, k, v):
    B, S, _ = q[0].shape
    nt = S // ATT_TILE

    def view(t, dil):
        return t.reshape(B, nt, dil, ATT_TILE // dil, AT_WIDTH)

    def main(dil):
        return pl.BlockSpec((None, None, dil, ATT_TILE // dil, AT_WIDTH),
                            lambda b, i: (b, i, 0, 0, 0))

    def prev(dil):
        lastblk = ATT_TILE // dil // BAND - 1
        return pl.BlockSpec((None, None, dil, BAND, AT_WIDTH),
                            lambda b, i: (b, jnp.maximum(i - 1, 0), 0, lastblk, 0))

    def nxt(dil):
        return pl.BlockSpec((None, None, dil, BAND, AT_WIDTH),
                            lambda b, i: (b, jnp.minimum(i + 1, nt - 1), 0, 0, 0))

    qs = [view(t, d) for t, d in zip(q, DILATIONS)]
    ks = [view(t, d) for t, d in zip(k, DILATIONS)]
    vs = [view(t, d) for t, d in zip(v, DILATIONS)]
    halo_specs, halo_args = [], []
    for ts_ in (ks, vs):
        for t, d in zip(ts_, DILATIONS):
            halo_specs += [prev(d), main(d), nxt(d)]
            halo_args += [t, t, t]
    return pl.pallas_call(
        functools.partial(_attn_kernel, nt=nt),
        grid=(B, nt),
        in_specs=[pl.BlockSpec((1, QBLK + 2 * BAND), lambda b, i: (0, 0))]
                 + [main(d) for d in DILATIONS] + halo_specs,
        out_specs=pl.BlockSpec((None, ATT_TILE, AT_WIDTH), lambda b, i: (b, i, 0)),
        out_shape=jax.ShapeDtypeStruct((B, S, AT_WIDTH), BF16),
        scratch_shapes=[pltpu.VMEM((ATT_TILE, LANES), F32)] * 4
                       + [pltpu.VMEM((4, QBLK, QBLK + 2 * BAND), F32)],
        compiler_params=_cparams("parallel", "parallel"),
        name="dilated_attn",
    )(negm, *qs, *halo_args)


def _out_kernel(x_ref, nw_ref, wg_ref, osum_ref, qtb_ref, lst_ref, dtile_ref, oat_ref, mq_ref,
                mk_ref, mva_ref, mvb_ref, negm_ref, onw_ref, w_ref, y_ref, sin_ref):
    nt_dims = (((1,), (1,)), ((), ()))

    @pl.when(pl.program_id(1) == 0)
    def _():
        sin_ref[...] = jnp.zeros_like(sin_ref)

    h = _rms_norm_bf16(x_ref[...], nw_ref[...])
    z_g = jnp.dot(h, wg_ref[...], preferred_element_type=F32)
    mem = [_pair_attend(mq_ref[:, cols], mk_ref[:, cols], mva_ref[:, cols], mvb_ref[:, cols],
                        negm_ref[...])
           for cols in (slice(0, LANES), slice(LANES, 2 * LANES))]
    n_sub = lst_ref.shape[0]
    corr = []
    for hd in range(HG_HEADS):
        cols = slice(hd * HG_D, (hd + 1) * HG_D)
        st = sin_ref[hd]
        pieces = [None] * n_sub
        for sub in reversed(range(n_sub)):
            rows = slice(sub * PROJ_TILE, (sub + 1) * PROJ_TILE)
            pieces[sub] = lax.dot_general(qtb_ref[rows, cols], st.astype(BF16), nt_dims,
                                          preferred_element_type=F32)
            st = st * dtile_ref[sub][:, cols] + lst_ref[sub, hd]
        sin_ref[hd] = st
        corr.append(jnp.concatenate(pieces, axis=0))
    parts = []

    def hgrn_out(o_sum, cr, zg, w):
        t = o_sum.astype(F32) + cr
        ms = jnp.mean(t * t, axis=-1, keepdims=True)
        return (t * lax.rsqrt(ms + NORM_EPS) * w * (zg * _sigmoid(zg))).astype(BF16)

    for hd in range(HG_HEADS):
        cols = slice(hd * HG_D, (hd + 1) * HG_D)
        parts.append(_slabs(hgrn_out, osum_ref[:, cols], corr[hd], z_g[:, cols], onw_ref[...]))
    c0 = HG_WIDTH
    parts.append(_slabs(lambda o, zg: (o.astype(F32) * (zg * _sigmoid(zg))).astype(BF16),
                        oat_ref[...], z_g[:, c0:c0 + AT_WIDTH]))
    c0 += AT_WIDTH
    for hp, (num, den) in enumerate(mem):
        parts.append(_slabs(lambda n, d, zg: (n / d * (zg * _sigmoid(zg))).astype(BF16),
                            num, den, z_g[:, c0 + hp * LANES:c0 + (hp + 1) * LANES]))
    mixed = jnp.concatenate(parts, axis=-1)
    y_ref[...] = x_ref[...] + jnp.dot(mixed, w_ref[...], preferred_element_type=F32)


def _out_stage(x, nw, w_in, o_sum, qtb, lst, dtile, o_at, mq, mk, mva, mvb, negm, onw, w_out):
    B, S, _ = x.shape
    ts = OUT_TILE
    nt = S // ts
    n_sub = OUT_TILE // PROJ_TILE
    row = lambda n: pl.BlockSpec((None, ts, n), lambda b, i: (b, nt - 1 - i, 0))
    memspec = pl.BlockSpec((None, MEM_TOKENS, MEM_WIDTH), lambda b, i: (b, 0, 0))
    return pl.pallas_call(
        _out_kernel,
        grid=(B, nt),
        in_specs=[row(D_MODEL), pl.BlockSpec((1, D_MODEL), lambda b, i: (0, 0)),
                  pl.BlockSpec((D_MODEL, IN_WIDTH - C_GATE),
                               lambda b, i: (0, C_GATE // (IN_WIDTH - C_GATE))),
                  row(HG_WIDTH), row(HG_WIDTH),
                  pl.BlockSpec((None, n_sub, HG_HEADS, HG_D, HG_D),
                               lambda b, i: (b, nt - 1 - i, 0, 0, 0)),
                  pl.BlockSpec((None, n_sub, 1, HG_WIDTH), lambda b, i: (b, nt - 1 - i, 0, 0)),
                  row(AT_WIDTH), row(MEM_WIDTH), memspec, memspec, memspec,
                  pl.BlockSpec((1, MEM_TOKENS), lambda b, i: (0, 0)),
                  pl.BlockSpec((1, HG_D), lambda b, i: (0, 0)),
                  pl.BlockSpec((D_MODEL, D_MODEL), lambda b, i: (0, 0))],
        out_specs=row(D_MODEL),
        out_shape=jax.ShapeDtypeStruct((B, S, D_MODEL), F32),
        scratch_shapes=[pltpu.VMEM((HG_HEADS, HG_D, HG_D), F32)],
        compiler_params=_cparams("parallel", "arbitrary"),
        name="out_stage",
    )(x, nw, w_in, o_sum, qtb, lst, dtile, o_at, mq, mk, mva, mvb, negm, onw, w_out)


def _rope_tables(S):
    half = ROPE_DIM // 2
    pos = jnp.arange(S, dtype=F32)
    inv_freq = ROPE_THETA ** (-jnp.arange(half, dtype=F32) * 2.0 / ROPE_DIM)
    ang = pos[:, None] * inv_freq[None, :]
    cos, sin = jnp.cos(ang), jnp.sin(ang)
    pad = HEAD64 - ROPE_DIM
    cos64 = jnp.concatenate([cos, cos, jnp.ones((S, pad), F32)], axis=-1)
    sin64 = jnp.concatenate([-sin, sin, jnp.zeros((S, pad), F32)], axis=-1)
    return jnp.tile(cos64, (1, 2)), jnp.tile(sin64, (1, 2))


def _lower_bounds(p):
    sm = jax.nn.softmax(p.astype(F32), axis=0)
    return jnp.cumsum(sm, axis=0) - sm[0:1]


def _tile4(w):
    return jnp.tile(w.astype(F32), (1, 4))


def _neg_stabiliser(wq, wk, width):
    bound = (HEAD64 ** 0.5 * LOG2E) * jnp.max(jnp.abs(wq), axis=-1) * jnp.max(jnp.abs(wk), axis=-1)
    bound = jnp.minimum(bound.astype(F32), EXP2_RANGE)
    return jnp.broadcast_to(-bound[:, None, None], (DEPTH, 1, width))


def _trunk(x, mem, prm):
    B, S, _ = x.shape
    assert S % ATT_TILE == 0 and ATT_TILE % PROJ_TILE == 0 and PROJ_TILE % SCAN_CHUNK == 0
    assert S % OUT_TILE == 0 and OUT_TILE % PROJ_TILE == 0
    cos_t, sin_t = _rope_tables(S)
    for l in range(DEPTH):
        p = {k: v[l] for k, v in prm.items()}
        (o_sum, qtb, lst, dtile, q1, q4, q16, k1, k4, k16, v1, v4, v16, mq) = _front(
            x, p["norm_w"], p["w_in"], p["lbf"], p["lbb"], p["aqw"], p["akw"], p["mqw"],
            cos_t, sin_t)
        mk, mva, mvb = _mem_kv(mem, p["mem_norm_w"], p["mem_wkv"], p["mkw"])
        o_at = _attention(p["negm_at"], (q1, q4, q16), (k1, k4, k16), (v1, v4, v16))
        x = _out_stage(x, p["norm_w"], p["w_in"], o_sum, qtb, lst, dtile, o_at, mq, mk, mva, mvb,
                       p["negm_mem"], p["onw"], p["w_out"])
    return x


def kernel(x_prompt, x_sample, mem_prompt, mem_sample, norm_w, w_in, hgrn_lb_fwd, hgrn_lb_bwd,
           hgrn_onorm_w, attn_qnorm_w, attn_knorm_w, mem_norm_w, mem_wkv, mem_qnorm_w,
           mem_knorm_w, w_out):
    prm = {
        "norm_w": norm_w.astype(F32)[:, None, :],
        "w_in": w_in.astype(BF16),
        "lbf": _lower_bounds(hgrn_lb_fwd)[:, None, :],
        "lbb": _lower_bounds(hgrn_lb_bwd)[:, None, :],
        "onw": hgrn_onorm_w.astype(F32)[:, None, :],
        "aqw": _tile4(attn_qnorm_w)[:, None, :],
        "akw": _tile4(attn_knorm_w)[:, None, :],
        "mem_norm_w": mem_norm_w.astype(F32)[:, None, :],
        "mem_wkv": mem_wkv.astype(BF16),
        "mqw": _tile4(mem_qnorm_w)[:, None, :],
        "mkw": _tile4(mem_knorm_w)[:, None, :],
        "w_out": w_out.astype(BF16),
        "negm_at": _neg_stabiliser(attn_qnorm_w, attn_knorm_w, QBLK + 2 * BAND),
        "negm_mem": _neg_stabiliser(mem_qnorm_w, mem_knorm_w, MEM_TOKENS),
    }
    y_prompt = _trunk(x_prompt, mem_prompt, prm)
    y_sample = _trunk(x_sample, mem_sample, prm)
    return (y_prompt, y_sample)
```
